```python
import jax, jax.numpy as jnp
from jax import lax
import numpy as np


D_MODEL = 2048
BATCH = 1
SEQ = 8192
DEPTH = 1
DEC_BATCH = 32
DEC_SEQ = 4
PAST_LEN = 16384
PAGE_SIZE = 128

N_MEM = 256
RET_HEADS = 8
RET_DK = 128
RET_DV = 256
RET_CHUNK = 128
ROPE_BASE = 10000.0
MOBA_HEADS = 8
MOBA_DH = 128
MOBA_BLOCK = 256
MOBA_TOPK = 3
Q_BLOCK = 128
MEM_HEADS = 4
MEM_DH = 256
D_FF = 4 * D_MODEL
N_BRANCH = 3
EPS = 1e-6
GN_EPS = 1e-5
NEG = -1e30
PROJ_WIDTHS = (RET_HEADS * RET_DK, RET_HEADS * RET_DK, RET_HEADS * RET_DV, RET_HEADS * RET_DV,
               MOBA_HEADS * MOBA_DH, MOBA_HEADS * MOBA_DH, MOBA_HEADS * MOBA_DH,
               MEM_HEADS * MEM_DH, N_BRANCH * D_MODEL)

kernel_name = 'retention_moba_memory_hybrid_step'

F32 = jnp.float32


def _rmsnorm(x, w):
    xf = x.astype(F32)
    y = xf * lax.rsqrt(jnp.mean(xf * xf, axis=-1, keepdims=True) + EPS)
    return (y * w.astype(F32)).astype(x.dtype)


def _rope(x, pos):
    d = x.shape[-1]
    half = d // 2
    inv = ROPE_BASE ** (-jnp.arange(half, dtype=F32) * (2.0 / d))
    ang = pos[:, None] * inv[None, :]
    cos = jnp.cos(ang)[None, :, None, :]
    sin = jnp.sin(ang)[None, :, None, :]
    xf = x.astype(F32)
    x1, x2 = xf[..., :half], xf[..., half:]
    return jnp.concatenate([x1 * cos - x2 * sin, x1 * sin + x2 * cos], axis=-1)


def _project(x, pos, attn_norm_w, w_in, moba_q_norm, moba_k_norm, mem_q_norm):
    B, T, _ = x.shape
    xn = _rmsnorm(x, attn_norm_w)
    proj = xn @ w_in
    splits = np.cumsum(PROJ_WIDTHS)[:-1].tolist()
    rq, rk, rv, rg, mq, mk, mv, cq, gates = jnp.split(proj, splits, axis=-1)
    rq = _rope(rq.reshape(B, T, RET_HEADS, RET_DK), pos)
    rk = _rope(rk.reshape(B, T, RET_HEADS, RET_DK), pos) * (RET_DK ** -0.5)
    rv = rv.reshape(B, T, RET_HEADS, RET_DV)
    mq = _rmsnorm(mq.reshape(B, T, MOBA_HEADS, MOBA_DH), moba_q_norm)
    mk = _rmsnorm(mk.reshape(B, T, MOBA_HEADS, MOBA_DH), moba_k_norm)
    mv = mv.reshape(B, T, MOBA_HEADS, MOBA_DH)
    cq = _rmsnorm(cq.reshape(B, T, MEM_HEADS, MEM_DH), mem_q_norm)
    return rq, rk, rv, rg, mq, mk, mv, cq, gates


def _retention(q, k, v, s0, chunk):
    B, T, H, dk = q.shape
    dv = v.shape[-1]
    nc = T // chunk
    log_g = jnp.log1p(-jnp.exp2(-5.0 - jnp.arange(H, dtype=F32)))
    i = jnp.arange(chunk, dtype=F32)
    diff = i[:, None] - i[None, :]
    dmask = jnp.where(diff >= 0, jnp.exp(jnp.maximum(diff, 0.0)[None] * log_g[:, None, None]), 0.0)
    q_decay = jnp.exp((i + 1.0)[:, None] * log_g[None, :])
    k_decay = jnp.exp((chunk - 1.0 - i)[:, None] * log_g[None, :])
    c_decay = jnp.exp(chunk * log_g)

    def to_chunks(a):
        return a.astype(F32).reshape(B, nc, chunk, H, a.shape[-1]).transpose(1, 0, 2, 3, 4)

    def step(S, inp):
        qi, ki, vi = inp
        s = jnp.einsum('bqhd,bkhd->bhqk', qi, ki) * dmask[None]
        o = (jnp.einsum('bhqk,bkhv->bqhv', s, vi)
             + jnp.einsum('bqhd,bhdv->bqhv', qi, S) * q_decay[None, :, :, None])
        S = S * c_decay[None, :, None, None] + jnp.einsum('bkhd,bkhv->bhdv', ki * k_decay[None, :, :, None], vi)
        return S, o

    S, o = lax.scan(step, s0.astype(F32), (to_chunks(q), to_chunks(k), to_chunks(v)))
    o = o.transpose(1, 0, 2, 3, 4).reshape(B, T, H, dv)
    return o, S


def _moba_prompt(q, k, v):
    B, T, H, d = q.shape
    nb = -(-T // MOBA_BLOCK)
    pad = nb * MOBA_BLOCK - T
    k_pad = jnp.pad(k, ((0, 0), (0, pad), (0, 0), (0, 0)))
    v_pad = jnp.pad(v, ((0, 0), (0, pad), (0, 0), (0, 0)))
    k_blocks = k_pad.reshape(B, nb, MOBA_BLOCK, H, d)
    v_blocks = v_pad.reshape(B, nb, MOBA_BLOCK, H, d)
    k_mean = jnp.mean(k_blocks.astype(F32), axis=2)
    topk = min(MOBA_TOPK, nb)
    scale = d ** -0.5
    bidx = jnp.arange(B)[:, None, None, None]
    hidx = jnp.arange(H)[None, None, :, None]

    def one_block(qb):
        start = qb * Q_BLOCK
        qi = lax.dynamic_slice_in_dim(q, start, Q_BLOCK, axis=1).astype(F32)
        t = start + jnp.arange(Q_BLOCK)
        blk = start // MOBA_BLOCK
        gate = jnp.einsum('bqhd,bjhd->bqhj', qi, k_mean)
        gate = jnp.where(jnp.arange(nb) < blk, gate, NEG)
        _, idx = lax.top_k(gate, topk)
        valid = idx < blk
        k_sel = k_blocks[bidx, idx, :, hidx]
        v_sel = v_blocks[bidx, idx, :, hidx]
        s_sel = jnp.einsum('bqhd,bqhnsd->bqhns', qi, k_sel.astype(F32)) * scale
        s_sel = jnp.where(valid[..., None], s_sel, NEG).reshape(B, Q_BLOCK, H, topk * MOBA_BLOCK)
        k_own = lax.dynamic_slice_in_dim(k_pad, blk * MOBA_BLOCK, MOBA_BLOCK, axis=1)
        v_own = lax.dynamic_slice_in_dim(v_pad, blk * MOBA_BLOCK, MOBA_BLOCK, axis=1)
        own_pos = blk * MOBA_BLOCK + jnp.arange(MOBA_BLOCK)
        s_own = jnp.einsum('bqhd,bkhd->bqhk', qi, k_own.astype(F32)) * scale
        s_own = jnp.where((own_pos[None, :] <= t[:, None])[None, :, None, :], s_own, NEG)
        p = jax.nn.softmax(jnp.concatenate([s_sel, s_own], axis=-1), axis=-1)
        p_sel = p[..., :topk * MOBA_BLOCK].reshape(B, Q_BLOCK, H, topk, MOBA_BLOCK)
        p_own = p[..., topk * MOBA_BLOCK:]
        return (jnp.einsum('bqhns,bqhnsd->bqhd', p_sel, v_sel.astype(F32))
                + jnp.einsum('bqhk,bkhd->bqhd', p_own, v_own.astype(F32)))

    out = lax.map(one_block, jnp.arange(T // Q_BLOCK))
    return out.transpose(1, 0, 2, 3, 4).reshape(B, T, H, d)


def _moba_sample(q, k_new, v_new, cache_k, cache_v, page_table):
    DB, T, H, d = q.shape
    n_pages = PAST_LEN // PAGE_SIZE
    ppb = MOBA_BLOCK // PAGE_SIZE
    blk = PAST_LEN // MOBA_BLOCK
    n_full = blk * ppb
    n_own = n_pages - n_full
    scale = d ** -0.5
    qf = q.astype(F32)
    scores, parts = [], []
    if blk > 0:
        topk = min(MOBA_TOPK, blk)
        k_full = cache_k[page_table[:, :n_full]].astype(F32)
        k_mean = jnp.mean(k_full.reshape(DB, blk, MOBA_BLOCK, H, d), axis=2)
        gate = jnp.einsum('bqhd,bjhd->bqhj', qf, k_mean)
        _, idx = lax.top_k(gate, topk)
        logical = idx[..., None] * ppb + jnp.arange(ppb)
        phys = page_table[jnp.arange(DB)[:, None, None, None, None], logical]
        hidx = jnp.arange(H)[None, None, :, None, None]
        k_sel = cache_k[phys, :, hidx].reshape(DB, T, H, topk * MOBA_BLOCK, d)
        v_sel = cache_v[phys, :, hidx].reshape(DB, T, H, topk * MOBA_BLOCK, d)
        scores.append(jnp.einsum('bqhd,bqhkd->bqhk', qf, k_sel.astype(F32)) * scale)
        parts.append(('bqhk,bqhkd->bqhd', v_sel))
    if n_own > 0:
        k_own = cache_k[page_table[:, n_full:]].reshape(DB, n_own * PAGE_SIZE, H, d)
        v_own = cache_v[page_table[:, n_full:]].reshape(DB, n_own * PAGE_SIZE, H, d)
        scores.append(jnp.einsum('bqhd,bkhd->bqhk', qf, k_own.astype(F32)) * scale)
        parts.append(('bqhk,bkhd->bqhd', v_own))
    causal = jnp.tril(jnp.ones((T, T), dtype=bool))
    s_new = jnp.einsum('bqhd,bkhd->bqhk', qf, k_new.astype(F32)) * scale
    scores.append(jnp.where(causal[None, :, None, :], s_new, NEG))
    parts.append(('bqhk,bkhd->bqhd', v_new))
    p = jax.nn.softmax(jnp.concatenate(scores, axis=-1), axis=-1)
    out = jnp.zeros((DB, T, H, d), F32)
    off = 0
    for s, (spec, vals) in zip(scores, parts):
        L = s.shape[-1]
        out = out + jnp.einsum(spec, p[..., off:off + L], vals.astype(F32))
        off += L
    return out


def _mem_kv(mem, mem_norm_w, w_mem_kv, mem_k_norm):
    B, M, _ = mem.shape
    kv = _rmsnorm(mem, mem_norm_w) @ w_mem_kv
    k, v = jnp.split(kv, 2, axis=-1)
    k = _rmsnorm(k.reshape(B, M, MEM_HEADS, MEM_DH), mem_k_norm)
    return k, v.reshape(B, M, MEM_HEADS, MEM_DH)


def _mem_attend(cq, mem_k, mem_v):
    s = jnp.einsum('bqhd,bkhd->bhqk', cq.astype(F32), mem_k.astype(F32)) * (MEM_DH ** -0.5)
    p = jax.nn.softmax(s, axis=-1)
    return jnp.einsum('bhqk,bkhd->bqhd', p, mem_v.astype(F32))


def _merge_and_mlp(x, ret_o, rg, moba_o, mem_o, gates, ret_gn_w, w_ret_o, w_moba_o, w_mem_o,
                   w_out, mlp_norm_w, w_up, w_down):
    B, T, _ = x.shape
    dt = x.dtype
    mu = jnp.mean(ret_o, axis=-1, keepdims=True)
    var = jnp.mean(jnp.square(ret_o - mu), axis=-1, keepdims=True)
    rn = ((ret_o - mu) * lax.rsqrt(var + GN_EPS)).reshape(B, T, -1) * ret_gn_w.astype(F32)
    ret_b = (jax.nn.silu(rg.astype(F32)) * rn).astype(dt) @ w_ret_o
    moba_b = moba_o.reshape(B, T, -1).astype(dt) @ w_moba_o
    mem_b = mem_o.reshape(B, T, -1).astype(dt) @ w_mem_o
    g = jax.nn.sigmoid(gates.astype(F32)).reshape(B, T, N_BRANCH, D_MODEL)
    merged = g[:, :, 0] * ret_b + g[:, :, 1] * moba_b + g[:, :, 2] * mem_b
    h = x + merged.astype(dt) @ w_out
    u = jax.nn.relu(_rmsnorm(h, mlp_norm_w) @ w_up)
    return h + (u * u) @ w_down


def setup_inputs(seed: int = 0) -> dict:
    key = jax.random.key(seed)
    ks = jax.random.split(key, 32)
    n_pages = PAST_LEN // PAGE_SIZE
    n_used = DEC_BATCH * n_pages
    n_phys = n_used + max(1, n_used // 4)
    nrm = jax.random.normal

    def gain(k, n):
        return 1.0 + 0.02 * nrm(k, (n,), F32)

    perm = jax.random.permutation(ks[0], n_phys).astype(jnp.int32)
    page_table = perm[:n_used].reshape(DEC_BATCH, n_pages)
    w_in_cols = int(sum(PROJ_WIDTHS))
    return {
        'x_prompt': nrm(ks[1], (BATCH, SEQ, D_MODEL), F32),
        'x_sample': nrm(ks[2], (DEC_BATCH, DEC_SEQ, D_MODEL), F32),
        'cache_k': nrm(ks[3], (n_phys, PAGE_SIZE, MOBA_HEADS, MOBA_DH), F32),
        'cache_v': nrm(ks[4], (n_phys, PAGE_SIZE, MOBA_HEADS, MOBA_DH), F32),
        'cache_mem_k': nrm(ks[5], (DEC_BATCH, N_MEM, MEM_HEADS, MEM_DH), F32),
        'cache_mem_v': nrm(ks[6], (DEC_BATCH, N_MEM, MEM_HEADS, MEM_DH), F32),
        'state_ret': 0.1 * nrm(ks[7], (DEC_BATCH, RET_HEADS, RET_DK, RET_DV), F32),
        'page_table': page_table,
        'mem_prompt': nrm(ks[8], (BATCH, N_MEM, D_MODEL), F32),
        'attn_norm_w': gain(ks[9], D_MODEL),
        'w_in': nrm(ks[10], (D_MODEL, w_in_cols), F32) * D_MODEL ** -0.5,
        'moba_q_norm': gain(ks[11], MOBA_DH),
        'moba_k_norm': gain(ks[12], MOBA_DH),
        'mem_q_norm': gain(ks[13], MEM_DH),
        'mem_k_norm': gain(ks[14], MEM_DH),
        'mem_norm_w': gain(ks[15], D_MODEL),
        'w_mem_kv': nrm(ks[16], (D_MODEL, 2 * MEM_HEADS * MEM_DH), F32) * D_MODEL ** -0.5,
        'ret_gn_w': gain(ks[17], RET_HEADS * RET_DV),
        'w_ret_o': nrm(ks[18], (RET_HEADS * RET_DV, D_MODEL), F32) * (RET_HEADS * RET_DV) ** -0.5,
        'w_moba_o': nrm(ks[19], (MOBA_HEADS * MOBA_DH, D_MODEL), F32) * (MOBA_HEADS * MOBA_DH) ** -0.5,
        'w_mem_o': nrm(ks[20], (MEM_HEADS * MEM_DH, D_MODEL), F32) * (MEM_HEADS * MEM_DH) ** -0.5,
        'w_out': nrm(ks[21], (D_MODEL, D_MODEL), F32) * D_MODEL ** -0.5,
        'mlp_norm_w': gain(ks[22], D_MODEL),
        'w_up': nrm(ks[23], (D_MODEL, D_FF), F32) * D_MODEL ** -0.5,
        'w_down': nrm(ks[24], (D_FF, D_MODEL), F32) * D_FF ** -0.5,
    }


def reference(x_prompt, x_sample, cache_k, cache_v, cache_mem_k, cache_mem_v, state_ret, page_table,
              mem_prompt, attn_norm_w, w_in, moba_q_norm, moba_k_norm, mem_q_norm, mem_k_norm,
              mem_norm_w, w_mem_kv, ret_gn_w, w_ret_o, w_moba_o, w_mem_o, w_out, mlp_norm_w,
              w_up, w_down):
    Bp, Tp, _ = x_prompt.shape
    pos_p = jnp.arange(Tp, dtype=F32)
    rq, rk, rv, rg, mq, mk_p, mv_p, cq, gates = _project(
        x_prompt, pos_p, attn_norm_w, w_in, moba_q_norm, moba_k_norm, mem_q_norm)
    s0 = jnp.zeros((Bp, RET_HEADS, RET_DK, RET_DV), F32)
    ret_o, ret_state_p = _retention(rq, rk, rv, s0, min(RET_CHUNK, Tp))
    moba_o = _moba_prompt(mq, mk_p, mv_p)
    mem_k_p, mem_v_p = _mem_kv(mem_prompt, mem_norm_w, w_mem_kv, mem_k_norm)
    mem_o = _mem_attend(cq, mem_k_p, mem_v_p)
    y_prompt = _merge_and_mlp(x_prompt, ret_o, rg, moba_o, mem_o, gates, ret_gn_w, w_ret_o, w_moba_o,
                              w_mem_o, w_out, mlp_norm_w, w_up, w_down)

    Ts = x_sample.shape[1]
    pos_s = float(PAST_LEN) + jnp.arange(Ts, dtype=F32)
    rq_s, rk_s, rv_s, rg_s, mq_s, mk_s, mv_s, cq_s, gates_s = _project(
        x_sample, pos_s, attn_norm_w, w_in, moba_q_norm, moba_k_norm, mem_q_norm)
    ret_o_s, ret_state_s = _retention(rq_s, rk_s, rv_s, state_ret, Ts)
    moba_o_s = _moba_sample(mq_s, mk_s, mv_s, cache_k, cache_v, page_table)
    mem_o_s = _mem_attend(cq_s, cache_mem_k, cache_mem_v)
    y_sample = _merge_and_mlp(x_sample, ret_o_s, rg_s, moba_o_s, mem_o_s, gates_s, ret_gn_w, w_ret_o,
                              w_moba_o, w_mem_o, w_out, mlp_norm_w, w_up, w_down)

    return (y_prompt, y_sample, mk_p, mv_p, ret_state_p.astype(state_ret.dtype), mem_k_p, mem_v_p,
            mk_s, mv_s, ret_state_s.astype(state_ret.dtype))
```

```python
import functools

import jax
import jax.numpy as jnp
from jax import lax
from jax.experimental import pallas as pl
from jax.experimental.pallas import tpu as pltpu

F32 = jnp.float32
BF16 = jnp.bfloat16

D_MODEL = 2048
SEQ = 8192
DEC_BATCH = 32
DEC_SEQ = 4
DEC_PAD = 8
PAST_LEN = 16384
PAGE_SIZE = 128
N_MEM = 256
RET_HEADS = 8
RET_DK = 128
RET_DV = 256
RET_CHUNK = 128
ROPE_BASE = 10000.0
MOBA_HEADS = 8
MOBA_DH = 128
MOBA_BLOCK = 256
MOBA_TOPK = 3
MEM_HEADS = 4
MEM_DH = 256
D_FF = 4 * D_MODEL
EPS = 1e-6
GN_EPS = 1e-5
NEG = -1e30

ROWS_S = DEC_BATCH * DEC_PAD
ROWS = SEQ + ROWS_S
TM = 1056
TN = 512
LANES = 128
VMEM_LIMIT = 56 * 1024 * 1024

C_RQ, C_RK, C_RV, C_RG = 0, 1024, 2048, 4096
C_MQ, C_MK, C_MV, C_CQ, C_GATE = 6144, 7168, 8192, 9216, 10240

_NT = (((1,), (1,)), ((), ()))


def _iota(shape, dim):
    return lax.broadcasted_iota(jnp.int32, shape, dim)


def _params(*sem):
    return pltpu.CompilerParams(dimension_semantics=sem, vmem_limit_bytes=VMEM_LIMIT)


def _rmsnorm_cast(x, w, tm):
    rows, d = x.shape

    def body(x_ref, w_ref, o_ref):
        xf = x_ref[...]
        y = xf * lax.rsqrt(jnp.mean(xf * xf, axis=-1, keepdims=True) + EPS)
        o_ref[...] = (y * w_ref[...]).astype(o_ref.dtype)

    return pl.pallas_call(
        body,
        grid=(rows // tm,),
        in_specs=[pl.BlockSpec((tm, d), lambda i: (i, 0)), pl.BlockSpec((1, d), lambda i: (0, 0))],
        out_specs=pl.BlockSpec((tm, d), lambda i: (i, 0)),
        out_shape=jax.ShapeDtypeStruct((rows, d), BF16),
        compiler_params=_params("arbitrary"),
        name="rmsnorm_cast",
    )(x, w.reshape(1, d))


def _rope_tables(inv):
    tm = TM

    def body(inv_ref, c_ref, s_ref):
        row = pl.program_id(0) * tm + _iota((tm, LANES), 0)
        pos = jnp.where(row < SEQ, row, PAST_LEN + ((row - SEQ) & (DEC_PAD - 1)))
        ang = pos.astype(F32) * inv_ref[...]
        sn = jnp.sin(ang)
        c_ref[...] = jnp.cos(ang)
        s_ref[...] = jnp.where(_iota((tm, LANES), 1) < RET_DK // 2, -sn, sn)

    return pl.pallas_call(
        body,
        grid=(ROWS // tm,),
        in_specs=[pl.BlockSpec((1, LANES), lambda i: (0, 0))],
        out_specs=[pl.BlockSpec((tm, LANES), lambda i: (i, 0))] * 2,
        out_shape=[jax.ShapeDtypeStruct((ROWS, LANES), F32)] * 2,
        compiler_params=_params("arbitrary"),
        name="rope_tables",
    )(inv)


def _matmul(a, w, col0, n_cols, tm, tn, epi_fn, epi_args, out_dtypes, name):
    m, k = a.shape
    assert m % tm == 0 and n_cols % tn == 0 and col0 % tn == 0
    jb0 = col0 // tn
    n_epi = len(epi_args)

    def body(a_ref, w_ref, *rest):
        epi_refs, out_refs = rest[:n_epi], rest[n_epi:]
        acc = jnp.dot(a_ref[...], w_ref[...], preferred_element_type=F32)
        for o_ref, o in zip(out_refs, epi_fn(acc, *epi_refs)):
            o_ref[...] = o.astype(o_ref.dtype)

    in_specs = [pl.BlockSpec((tm, k), lambda i, j: (i, 0)),
                pl.BlockSpec((k, tn), lambda i, j: (0, jb0 + j))]
    in_specs += [pl.BlockSpec(bs, im) for _, bs, im in epi_args]
    outs = pl.pallas_call(
        body,
        grid=(m // tm, n_cols // tn),
        in_specs=in_specs,
        out_specs=[pl.BlockSpec((tm, tn), lambda i, j: (i, j)) for _ in out_dtypes],
        out_shape=[jax.ShapeDtypeStruct((m, n_cols), dt) for dt in out_dtypes],
        compiler_params=_params("arbitrary", "arbitrary"),
        name=name,
    )(a, w, *[x for x, _, _ in epi_args])
    return outs[0] if len(outs) == 1 else outs


def _epi_plain(acc):
    return (acc,)


def _epi_sigmoid(acc):
    return (1.0 / (1.0 + jnp.exp(-acc)),)


def _epi_relu_sq(acc):
    u = jnp.maximum(acc, 0.0)
    return (u * u,)


def _epi_rope(scale, acc, c_ref, s_ref):
    c, s = c_ref[...], s_ref[...]
    pieces = []
    for hh in range(acc.shape[1] // RET_DK):
        x = acc[:, hh * RET_DK:(hh + 1) * RET_DK]
        y = x * c + pltpu.roll(x, RET_DK // 2, 1) * s
        pieces.append(y * scale if scale != 1.0 else y)
    return (jnp.concatenate(pieces, axis=1),)


def _epi_headnorm(hw, acc, w_ref):
    w = w_ref[...]
    pieces = []
    for hh in range(acc.shape[1] // hw):
        x = acc[:, hh * hw:(hh + 1) * hw]
        pieces.append(x * lax.rsqrt(jnp.mean(x * x, axis=-1, keepdims=True) + EPS) * w)
    return (jnp.concatenate(pieces, axis=1),)


def _epi_gate_first(acc, g_ref):
    return (g_ref[...] * acc,)


def _epi_gate_add(acc, g_ref, prev_ref):
    return (prev_ref[...] + g_ref[...] * acc,)


def _epi_residual(acc, x_ref):
    return (x_ref[...] + acc,)


def _gated_groupnorm(o, rg, gw):
    mu = jnp.mean(o, axis=-1, keepdims=True)
    d = o - mu
    var = jnp.mean(d * d, axis=-1, keepdims=True)
    rn = d * lax.rsqrt(var + GN_EPS) * gw
    return rg / (1.0 + jnp.exp(-rg)) * rn


def _retention_prompt(log_g, rq, rk, rv, rg, gn_w):
    C = RET_CHUNK
    nc = SEQ // C

    def body(lg_ref, q_ref, k_ref, v_ref, rg_ref, gw_ref, o_ref, st_ref, state):
        h, c = pl.program_id(0), pl.program_id(1)

        @pl.when(c == 0)
        def _():
            state[...] = jnp.zeros_like(state)

        lg = lg_ref[h]
        q, k, v = q_ref[...], k_ref[...], v_ref[...]
        diff = (_iota((C, C), 0) - _iota((C, C), 1)).astype(F32)
        dmask = jnp.where(diff >= 0, jnp.exp(jnp.maximum(diff, 0.0) * lg), 0.0)
        s = lax.dot_general(q, k, _NT, preferred_element_type=F32) * dmask
        ii = _iota((C, 1), 0).astype(F32)
        q_decay = jnp.exp((ii + 1.0) * lg)
        k_decay = jnp.exp((C - 1.0 - ii) * lg)
        c_decay = jnp.exp(jnp.zeros((1, RET_DV), F32) + C * lg)
        s_old = state[...]
        o = (jnp.dot(s.astype(BF16), v, preferred_element_type=F32)
             + jnp.dot(q, s_old.astype(BF16), preferred_element_type=F32) * q_decay)
        kd_t = (k.astype(F32) * k_decay).T.astype(BF16)
        s_new = s_old * c_decay + jnp.dot(kd_t, v, preferred_element_type=F32)
        state[...] = s_new

        @pl.when(c == nc - 1)
        def _():
            st_ref[0] = s_new

        o_ref[...] = _gated_groupnorm(o, rg_ref[...], gw_ref[...]).astype(o_ref.dtype)

    return pl.pallas_call(
        body,
        grid=(RET_HEADS, nc),
        in_specs=[
            pl.BlockSpec(memory_space=pltpu.SMEM),
            pl.BlockSpec((C, RET_DK), lambda h, c: (c, h)),
            pl.BlockSpec((C, RET_DK), lambda h, c: (c, h)),
            pl.BlockSpec((C, RET_DV), lambda h, c: (c, h)),
            pl.BlockSpec((C, RET_DV), lambda h, c: (c, h)),
            pl.BlockSpec((1, RET_DV), lambda h, c: (0, h)),
        ],
        out_specs=[
            pl.BlockSpec((C, RET_DV), lambda h, c: (c, h)),
            pl.BlockSpec((1, RET_DK, RET_DV), lambda h, c: (h, 0, 0)),
        ],
        out_shape=[
            jax.ShapeDtypeStruct((SEQ, RET_HEADS * RET_DV), BF16),
            jax.ShapeDtypeStruct((RET_HEADS, RET_DK, RET_DV), F32),
        ],
        scratch_shapes=[pltpu.VMEM((RET_DK, RET_DV), F32)],
        compiler_params=_params("arbitrary", "arbitrary"),
        name="retention_prompt",
    )(log_g, rq, rk, rv, rg, gn_w)


def _retention_sample(log_g, rq, rk, rv, rg, gn_w, state_ret):
    T = DEC_SEQ
    P = DEC_PAD
    nb = 2
    row_blk0 = SEQ // (nb * P)

    def body(lg_ref, q_ref, k_ref, v_ref, rg_ref, gw_ref, st_in_ref, o_ref, st_ref):
        qa, ka, va = q_ref[...].astype(F32), k_ref[...].astype(F32), v_ref[...].astype(F32)
        rga, gw = rg_ref[...], gw_ref[...]
        ri, ci = _iota((P, P), 0), _iota((P, P), 1)
        diff = (ri - ci).astype(F32)
        ii = _iota((P, 1), 0)
        iif = ii.astype(F32)
        live = ii < T
        out_rows = []
        for bb in range(nb):
            rows = slice(bb * P, (bb + 1) * P)
            out_heads = []
            for h in range(RET_HEADS):
                lg = lg_ref[h]
                q = qa[rows, h * RET_DK:(h + 1) * RET_DK]
                k = jnp.where(live, ka[rows, h * RET_DK:(h + 1) * RET_DK], 0.0)
                v = jnp.where(live, va[rows, h * RET_DV:(h + 1) * RET_DV], 0.0)
                dmask = jnp.where(diff >= 0, jnp.exp(jnp.maximum(diff, 0.0) * lg), 0.0)
                s = lax.dot_general(q, k, _NT, preferred_element_type=F32) * dmask
                q_decay = jnp.exp((iif + 1.0) * lg)
                k_decay = jnp.exp((T - 1.0 - iif) * lg)
                c_decay = jnp.exp(jnp.zeros((1, RET_DV), F32) + T * lg)
                s_old = st_in_ref[bb, h]
                o = (jnp.dot(s, v, preferred_element_type=F32)
                     + jnp.dot(q, s_old, preferred_element_type=F32) * q_decay)
                upd = lax.dot_general(k * k_decay, v, (((0,), (0,)), ((), ())),
                                      preferred_element_type=F32)
                st_ref[bb, h] = s_old * c_decay + upd
                out_heads.append(_gated_groupnorm(
                    o, rga[rows, h * RET_DV:(h + 1) * RET_DV], gw[:, h * RET_DV:(h + 1) * RET_DV]))
            out_rows.append(jnp.concatenate(out_heads, axis=1))
        o_ref[...] = jnp.concatenate(out_rows, axis=0).astype(o_ref.dtype)

    wq, wv = RET_HEADS * RET_DK, RET_HEADS * RET_DV
    return pl.pallas_call(
        body,
        grid=(DEC_BATCH // nb,),
        in_specs=[
            pl.BlockSpec(memory_space=pltpu.SMEM),
            pl.BlockSpec((nb * P, wq), lambda g: (row_blk0 + g, 0)),
            pl.BlockSpec((nb * P, wq), lambda g: (row_blk0 + g, 0)),
            pl.BlockSpec((nb * P, wv), lambda g: (row_blk0 + g, 0)),
            pl.BlockSpec((nb * P, wv), lambda g: (row_blk0 + g, 0)),
            pl.BlockSpec((1, wv), lambda g: (0, 0)),
            pl.BlockSpec((nb, RET_HEADS, RET_DK, RET_DV), lambda g: (g, 0, 0, 0)),
        ],
        out_specs=[
            pl.BlockSpec((nb * P, wv), lambda g: (g, 0)),
            pl.BlockSpec((nb, RET_HEADS, RET_DK, RET_DV), lambda g: (g, 0, 0, 0)),
        ],
        out_shape=[
            jax.ShapeDtypeStruct((ROWS_S, wv), BF16),
            jax.ShapeDtypeStruct((DEC_BATCH, RET_HEADS, RET_DK, RET_DV), F32),
        ],
        compiler_params=_params("arbitrary"),
        name="retention_sample",
    )(log_g, rq, rk, rv, rg, gn_w, state_ret)


def _top3_select(g, lane, always):
    sel = always
    for _ in range(MOBA_TOPK):
        mx = jnp.max(g, axis=-1, keepdims=True)
        cand = (g == mx) & (g > -jnp.inf)
        first = jnp.min(jnp.where(cand, lane, 1 << 20), axis=-1, keepdims=True)
        pick = lane == first
        sel = sel | pick
        g = jnp.where(pick, -jnp.inf, g)
    return sel


def _moba_prompt(mq, mk, mv):
    BL = MOBA_BLOCK
    nblk = SEQ // BL
    scale = MOBA_DH ** -0.5

    def body(q_ref, k_ref, v_ref, o_ref, kaug, vb, kmean):
        i = pl.program_id(1)

        @pl.when(i == 0)
        def _():
            kmean[...] = jnp.zeros_like(kmean)
            lane = _iota((BL, LANES), 1)

            def prep(jb, carry):
                r0 = pl.multiple_of(jb * BL, BL)
                kb = k_ref[pl.ds(r0, BL), :]
                kmean[pl.ds(jb, 1), :] = jnp.mean(kb, axis=0, keepdims=True)
                kaug[pl.ds(r0, BL), 0:MOBA_DH] = kb.astype(BF16)
                kaug[pl.ds(r0, BL), MOBA_DH:2 * MOBA_DH] = (lane == jb).astype(BF16)
                vb[pl.ds(r0, BL), :] = v_ref[pl.ds(r0, BL), :].astype(BF16)
                return carry

            lax.fori_loop(0, nblk, prep, 0)

        q = q_ref[...]
        gate = lax.dot_general(q, kmean[...], _NT, precision=lax.Precision.HIGHEST,
                               preferred_element_type=F32)
        lane = _iota((BL, LANES), 1)
        sel = _top3_select(jnp.where(lane < i, gate, -jnp.inf), lane, lane == i)
        bias = jnp.where(sel, 0.0, NEG)
        qa = jnp.concatenate([(q * scale).astype(BF16), bias.astype(BF16)], axis=1)

        r0 = pl.multiple_of(i * BL, BL)
        s = lax.dot_general(qa, kaug[pl.ds(r0, BL), :], _NT, preferred_element_type=F32)
        s = jnp.where(_iota((BL, BL), 1) <= _iota((BL, BL), 0), s, NEG)
        m = jnp.max(s, axis=-1, keepdims=True)
        p = jnp.exp(s - m)
        l = jnp.sum(p, axis=-1, keepdims=True)
        acc = jnp.dot(p.astype(BF16), vb[pl.ds(r0, BL), :], preferred_element_type=F32)

        def step(j, carry):
            m, l, acc = carry
            c0 = pl.multiple_of(j * BL, BL)
            s = lax.dot_general(qa, kaug[pl.ds(c0, BL), :], _NT, preferred_element_type=F32)
            m_new = jnp.maximum(m, jnp.max(s, axis=-1, keepdims=True))
            alpha = jnp.exp(m - m_new)
            p = jnp.exp(s - m_new)
            l = l * alpha + jnp.sum(p, axis=-1, keepdims=True)
            acc = acc * alpha + jnp.dot(p.astype(BF16), vb[pl.ds(c0, BL), :],
                                        preferred_element_type=F32)
            return m_new, l, acc

        m, l, acc = lax.fori_loop(0, i, step, (m, l, acc))
        o_ref[...] = (acc / l).astype(o_ref.dtype)

    return pl.pallas_call(
        body,
        grid=(MOBA_HEADS, nblk),
        in_specs=[
            pl.BlockSpec((BL, MOBA_DH), lambda h, i: (i, h)),
            pl.BlockSpec((SEQ, MOBA_DH), lambda h, i: (0, h)),
            pl.BlockSpec((SEQ, MOBA_DH), lambda h, i: (0, h)),
        ],
        out_specs=pl.BlockSpec((BL, MOBA_DH), lambda h, i: (i, h)),
        out_shape=jax.ShapeDtypeStruct((SEQ, MOBA_HEADS * MOBA_DH), BF16),
        scratch_shapes=[
            pltpu.VMEM((SEQ, 2 * MOBA_DH), BF16),
            pltpu.VMEM((SEQ, MOBA_DH), BF16),
            pltpu.VMEM((LANES, MOBA_DH), F32),
        ],
        compiler_params=_params("arbitrary", "arbitrary"),
        name="moba_prompt",
    )(mq, mk, mv)


def _sample_block_means(cache_k3, page_table):
    ppb = MOBA_BLOCK // PAGE_SIZE
    nblk = PAST_LEN // MOBA_BLOCK
    w = MOBA_HEADS * MOBA_DH

    def body(pt_ref, p0_ref, p1_ref, o_ref):
        j = pl.program_id(1)
        tot = (jnp.sum(p0_ref[0], axis=0, keepdims=True) + jnp.sum(p1_ref[0], axis=0, keepdims=True))
        o_ref[0, pl.ds(j, 1), :] = tot * (1.0 / MOBA_BLOCK)

    grid_spec = pltpu.PrefetchScalarGridSpec(
        num_scalar_prefetch=1,
        grid=(DEC_BATCH, nblk),
        in_specs=[
            pl.BlockSpec((1, PAGE_SIZE, w), lambda b, j, pt: (pt[b, ppb * j], 0, 0)),
            pl.BlockSpec((1, PAGE_SIZE, w), lambda b, j, pt: (pt[b, ppb * j + 1], 0, 0)),
        ],
        out_specs=pl.BlockSpec((1, nblk, w), lambda b, j, pt: (b, 0, 0)),
    )
    return pl.pallas_call(
        body,
        grid_spec=grid_spec,
        out_shape=jax.ShapeDtypeStruct((DEC_BATCH, nblk, w), F32),
        compiler_params=_params("arbitrary", "arbitrary"),
        name="sample_block_means",
    )(page_table, cache_k3, cache_k3)


def _sample_block_select(mq_s, kmean):
    nblk = PAST_LEN // MOBA_BLOCK
    P = DEC_PAD

    def body(q_ref, km_ref, o_ref):
        q = q_ref[...]
        km = km_ref[0]
        lane = _iota((P, LANES), 1)
        for h in range(MOBA_HEADS):
            cols = slice(h * MOBA_DH, (h + 1) * MOBA_DH)
            kmh = jnp.concatenate([km[:, cols], jnp.zeros((LANES - nblk, MOBA_DH), F32)], axis=0)
            gate = lax.dot_general(q[:, cols], kmh, _NT, precision=lax.Precision.HIGHEST,
                                   preferred_element_type=F32)
            g = jnp.where(lane < nblk, gate, -jnp.inf)
            ids = []
            for _ in range(MOBA_TOPK):
                mx = jnp.max(g, axis=-1, keepdims=True)
                first = jnp.min(jnp.where(g == mx, lane, 1 << 20), axis=-1, keepdims=True)
                ids.append(first)
                g = jnp.where(lane == first, -jnp.inf, g)
            o_ref[0, h] = jnp.where(lane == 0, ids[0], jnp.where(lane == 1, ids[1], ids[2]))

    return pl.pallas_call(
        body,
        grid=(DEC_BATCH,),
        in_specs=[
            pl.BlockSpec((P, MOBA_HEADS * MOBA_DH), lambda b: (b, 0)),
            pl.BlockSpec((1, nblk, MOBA_HEADS * MOBA_DH), lambda b: (b, 0, 0)),
        ],
        out_specs=pl.BlockSpec((1, MOBA_HEADS, P, LANES), lambda b: (b, 0, 0, 0)),
        out_shape=jax.ShapeDtypeStruct((DEC_BATCH, MOBA_HEADS, P, LANES), jnp.int32),
        compiler_params=_params("arbitrary"),
        name="sample_block_select",
    )(mq_s, kmean)


def _sample_attend(mq_s, mk_s, mv_s, cache_k3, cache_v3, page_table, blk_ids):
    ppb = MOBA_BLOCK // PAGE_SIZE
    n_sel = MOBA_TOPK * MOBA_BLOCK
    n_copies = DEC_SEQ * MOBA_TOPK * ppb
    scale = MOBA_DH ** -0.5
    P = DEC_PAD

    def body(pt_ref, ids_ref, q_ref, kn_ref, vn_ref, ck_ref, cv_ref, o_ref, kbuf, vbuf, sem):
        b, h = pl.program_id(0), pl.program_id(1)
        col = pl.multiple_of(h * MOBA_DH, MOBA_DH)
        copies = []
        for t in range(DEC_SEQ):
            for sl in range(MOBA_TOPK):
                blk = ids_ref[((b * DEC_SEQ + t) * MOBA_HEADS + h) * MOBA_TOPK + sl]
                for pg in range(ppb):
                    phys = pt_ref[b, blk * ppb + pg]
                    n = (t * MOBA_TOPK + sl) * ppb + pg
                    rows = pl.ds((sl * ppb + pg) * PAGE_SIZE, PAGE_SIZE)
                    for src, dst, which in ((ck_ref, kbuf, 0), (cv_ref, vbuf, 1)):
                        cp = pltpu.make_async_copy(src.at[phys, :, pl.ds(col, MOBA_DH)],
                                                   dst.at[t, rows, :], sem.at[which, n])
                        cp.start()
                        copies.append(cp)
        for cp in copies:
            cp.wait()

        q, kn, vn = q_ref[...], kn_ref[...], vn_ref[...]
        rown = _iota((P, 1), 0)
        for t in range(DEC_SEQ):
            qt = q[t:t + 1, :] * scale
            s = jnp.sum(kbuf[t] * qt, axis=-1, keepdims=True)
            sn = jnp.where(rown <= t, jnp.sum(kn * qt, axis=-1, keepdims=True), NEG)
            m = jnp.maximum(jnp.max(s, axis=0, keepdims=True), jnp.max(sn, axis=0, keepdims=True))
            p, pn = jnp.exp(s - m), jnp.exp(sn - m)
            l = jnp.sum(p, axis=0, keepdims=True) + jnp.sum(pn, axis=0, keepdims=True)
            o = (jnp.sum(p * vbuf[t], axis=0, keepdims=True) + jnp.sum(pn * vn, axis=0, keepdims=True))
            o_ref[pl.ds(t, 1), :] = o / l
        o_ref[pl.ds(DEC_SEQ, P - DEC_SEQ), :] = jnp.zeros((P - DEC_SEQ, MOBA_DH), F32)

    head_blk = pl.BlockSpec((P, MOBA_DH), lambda b, h, pt, ids: (b, h))
    grid_spec = pltpu.PrefetchScalarGridSpec(
        num_scalar_prefetch=2,
        grid=(DEC_BATCH, MOBA_HEADS),
        in_specs=[head_blk, head_blk, head_blk,
                  pl.BlockSpec(memory_space=pl.ANY), pl.BlockSpec(memory_space=pl.ANY)],
        out_specs=head_blk,
        scratch_shapes=[
            pltpu.VMEM((DEC_SEQ, n_sel, MOBA_DH), F32),
            pltpu.VMEM((DEC_SEQ, n_sel, MOBA_DH), F32),
            pltpu.SemaphoreType.DMA((2, n_copies)),
        ],
    )
    return pl.pallas_call(
        body,
        grid_spec=grid_spec,
        out_shape=jax.ShapeDtypeStruct((ROWS_S, MOBA_HEADS * MOBA_DH), F32),
        compiler_params=_params("arbitrary", "arbitrary"),
        name="sample_attend",
    )(page_table, blk_ids, mq_s, mk_s, mv_s, cache_k3, cache_v3)


def _softmax_attend(q, k, v):
    s = lax.dot_general(q, k, _NT, preferred_element_type=F32)
    m = jnp.max(s, axis=-1, keepdims=True)
    p = jnp.exp(s - m)
    l = jnp.sum(p, axis=-1, keepdims=True)
    return jnp.dot(p.astype(v.dtype), v, preferred_element_type=F32) / l


def _mem_attend_prompt(cq, mem_k, mem_v):
    tq = 512
    scale = MEM_DH ** -0.5

    def body(q_ref, k_ref, v_ref, o_ref):
        q = (q_ref[...] * scale).astype(BF16)
        o_ref[...] = _softmax_attend(q, k_ref[...].astype(BF16), v_ref[...].astype(BF16)).astype(o_ref.dtype)

    kv_spec = pl.BlockSpec((N_MEM, MEM_DH), lambda r, h: (0, h))
    return pl.pallas_call(
        body,
        grid=(SEQ // tq, MEM_HEADS),
        in_specs=[pl.BlockSpec((tq, MEM_DH), lambda r, h: (r, h)), kv_spec, kv_spec],
        out_specs=pl.BlockSpec((tq, MEM_DH), lambda r, h: (r, h)),
        out_shape=jax.ShapeDtypeStruct((SEQ, MEM_HEADS * MEM_DH), BF16),
        compiler_params=_params("arbitrary", "arbitrary"),
        name="mem_attend_prompt",
    )(cq, mem_k, mem_v)


def _mem_attend_sample(cq, mem_k, mem_v):
    nb = 2
    P = DEC_PAD
    w = MEM_HEADS * MEM_DH
    scale = MEM_DH ** -0.5
    row_blk0 = SEQ // (nb * P)

    def body(q_ref, k_ref, v_ref, o_ref):
        qa = q_ref[...] * scale
        out_rows = []
        for bb in range(nb):
            heads = []
            for h in range(MEM_HEADS):
                cols = slice(h * MEM_DH, (h + 1) * MEM_DH)
                heads.append(_softmax_attend(qa[bb * P:(bb + 1) * P, cols], k_ref[bb, :, cols],
                                             v_ref[bb, :, cols]))
            out_rows.append(jnp.concatenate(heads, axis=1))
        o_ref[...] = jnp.concatenate(out_rows, axis=0).astype(o_ref.dtype)

    kv_spec = pl.BlockSpec((nb, N_MEM, w), lambda g: (g, 0, 0))
    return pl.pallas_call(
        body,
        grid=(DEC_BATCH // nb,),
        in_specs=[pl.BlockSpec((nb * P, w), lambda g: (row_blk0 + g, 0)), kv_spec, kv_spec],
        out_specs=pl.BlockSpec((nb * P, w), lambda g: (g, 0)),
        out_shape=jax.ShapeDtypeStruct((ROWS_S, w), BF16),
        compiler_params=_params("arbitrary"),
        name="mem_attend_sample",
    )(cq, mem_k, mem_v)


def kernel(x_prompt, x_sample, cache_k, cache_v, cache_mem_k, cache_mem_v, state_ret, page_table, mem_prompt, attn_norm_w, w_in, moba_q_norm, moba_k_norm, mem_q_norm, mem_k_norm, mem_norm_w, w_mem_kv, ret_gn_w, w_ret_o, w_moba_o, w_mem_o, w_out, mlp_norm_w, w_up, w_down):
    n_phys = cache_k.shape[0]
    hd_moba = MOBA_HEADS * MOBA_DH
    hd_mem = MEM_HEADS * MEM_DH

    xs = jnp.pad(x_sample, ((0, 0), (0, DEC_PAD - DEC_SEQ), (0, 0))).reshape(ROWS_S, D_MODEL)
    x = jnp.concatenate([x_prompt[0], xs], axis=0)

    half = RET_DK // 2
    inv = ROPE_BASE ** (-jnp.arange(half, dtype=F32) * (2.0 / RET_DK))
    inv = jnp.concatenate([inv, inv]).reshape(1, LANES)
    log_g = jnp.log1p(-jnp.exp2(-5.0 - jnp.arange(RET_HEADS, dtype=F32)))
    cos_t, sin_t = _rope_tables(inv)

    w_in_b = w_in.astype(BF16)
    xn = _rmsnorm_cast(x, attn_norm_w, 384)

    rope_args = [(cos_t, (TM, LANES), lambda i, j: (i, 0)), (sin_t, (TM, LANES), lambda i, j: (i, 0))]

    def norm_arg(wv):
        return [(wv.reshape(1, -1), (1, wv.shape[0]), lambda i, j: (0, 0))]

    def proj(col0, n_cols, epi, args, dt, name):
        return _matmul(xn, w_in_b, col0, n_cols, TM, TN, epi, args, [dt], name)

    rq = proj(C_RQ, 1024, functools.partial(_epi_rope, 1.0), rope_args, BF16, "proj_rq")
    rk = proj(C_RK, 1024, functools.partial(_epi_rope, RET_DK ** -0.5), rope_args, BF16, "proj_rk")
    rv = proj(C_RV, 2048, _epi_plain, [], BF16, "proj_rv")
    rg = proj(C_RG, 2048, _epi_plain, [], F32, "proj_rg")
    mq = proj(C_MQ, 1024, functools.partial(_epi_headnorm, MOBA_DH), norm_arg(moba_q_norm), F32, "proj_mq")
    mk = proj(C_MK, 1024, functools.partial(_epi_headnorm, MOBA_DH), norm_arg(moba_k_norm), F32, "proj_mk")
    mv = proj(C_MV, 1024, _epi_plain, [], F32, "proj_mv")
    cq = proj(C_CQ, 1024, functools.partial(_epi_headnorm, MEM_DH), norm_arg(mem_q_norm), F32, "proj_cq")
    gates = proj(C_GATE, 3 * D_MODEL, _epi_sigmoid, [], F32, "proj_gates")

    mn = _rmsnorm_cast(mem_prompt[0], mem_norm_w, N_MEM)
    w_mem_b = w_mem_kv.astype(BF16)
    mem_k_p = _matmul(mn, w_mem_b, 0, hd_mem, N_MEM, TN, functools.partial(_epi_headnorm, MEM_DH),
                      norm_arg(mem_k_norm), [F32], "mem_k")
    mem_v_p = _matmul(mn, w_mem_b, hd_mem, hd_mem, N_MEM, TN, _epi_plain, [], [F32], "mem_v")

    gn_w = ret_gn_w.reshape(1, -1)
    ret_p, ret_state_p = _retention_prompt(log_g, rq, rk, rv, rg, gn_w)
    ret_s, ret_state_s = _retention_sample(log_g, rq, rk, rv, rg, gn_w, state_ret)

    mq_p, mk_p, mv_p = mq[:SEQ], mk[:SEQ], mv[:SEQ]
    mq_s, mk_s, mv_s = mq[SEQ:], mk[SEQ:], mv[SEQ:]
    moba_p = _moba_prompt(mq_p, mk_p, mv_p)
    cache_k3 = cache_k.reshape(n_phys, PAGE_SIZE, hd_moba)
    cache_v3 = cache_v.reshape(n_phys, PAGE_SIZE, hd_moba)
    kmean_s = _sample_block_means(cache_k3, page_table)
    ids = _sample_block_select(mq_s, kmean_s)
    ids = ids[:, :, :DEC_SEQ, :MOBA_TOPK].transpose(0, 2, 1, 3).reshape(-1)
    moba_s = _sample_attend(mq_s, mk_s, mv_s, cache_k3, cache_v3, page_table, ids)

    mem_p = _mem_attend_prompt(cq, mem_k_p, mem_v_p)
    mem_s = _mem_attend_sample(cq, cache_mem_k.reshape(DEC_BATCH, N_MEM, hd_mem),
                               cache_mem_v.reshape(DEC_BATCH, N_MEM, hd_mem))

    ret_o = jnp.concatenate([ret_p, ret_s], axis=0)
    moba_o = jnp.concatenate([moba_p, moba_s.astype(BF16)], axis=0)
    mem_o = jnp.concatenate([mem_p, mem_s], axis=0)

    def gate_arg(branch):
        off = branch * D_MODEL // TN
        return (gates, (TM, TN), lambda i, j: (i, off + j))

    def tile_arg(arr, tm=TM):
        return (arr, (tm, TN), lambda i, j: (i, j))

    merged = _matmul(ret_o, w_ret_o.astype(BF16), 0, D_MODEL, TM, TN, _epi_gate_first,
                     [gate_arg(0)], [F32], "branch_ret")
    merged = _matmul(moba_o, w_moba_o.astype(BF16), 0, D_MODEL, TM, TN, _epi_gate_add,
                     [gate_arg(1), tile_arg(merged)], [F32], "branch_moba")
    merged = _matmul(mem_o, w_mem_o.astype(BF16), 0, D_MODEL, TM, TN, _epi_gate_add,
                     [gate_arg(2), tile_arg(merged)], [BF16], "branch_mem")
    hres = _matmul(merged, w_out.astype(BF16), 0, D_MODEL, TM, TN, _epi_residual,
                   [tile_arg(x)], [F32], "out_proj")
    hn = _rmsnorm_cast(hres, mlp_norm_w, 384)
    u2 = _matmul(hn, w_up.astype(BF16), 0, D_FF, TM, TN, _epi_relu_sq, [], [BF16], "mlp_up")
    tm_down = TM // 2
    y = _matmul(u2, w_down.astype(BF16), 0, D_MODEL, tm_down, TN, _epi_residual,
                [tile_arg(hres, tm_down)], [F32], "mlp_down")

    def sample_rows(a):
        return a.reshape(DEC_BATCH, DEC_PAD, *a.shape[1:])[:, :DEC_SEQ]

    y_prompt = y[:SEQ][None]
    y_sample = sample_rows(y[SEQ:])
    shape4 = (MOBA_HEADS, MOBA_DH)
    return (
        y_prompt,
        y_sample,
        mk_p.reshape(1, SEQ, *shape4),
        mv_p.reshape(1, SEQ, *shape4),
        ret_state_p[None],
        mem_k_p.reshape(1, N_MEM, MEM_HEADS, MEM_DH),
        mem_v_p.reshape(1, N_MEM, MEM_HEADS, MEM_DH),
        sample_rows(mk_s).reshape(DEC_BATCH, DEC_SEQ, *shape4),
        sample_rows(mv_s).reshape(DEC_BATCH, DEC_SEQ, *shape4),
        ret_state_s,
    )
```

```python
import functools

import jax
import jax.numpy as jnp
from jax import lax
from jax.experimental import pallas as pl
from jax.experimental.pallas import tpu as pltpu

F32 = jnp.float32
BF16 = jnp.bfloat16

D_MODEL = 2048
SEQ = 8192
DEC_BATCH = 32
DEC_SEQ = 4
DEC_PAD = 8
PAST_LEN = 16384
PAGE_SIZE = 128
N_MEM = 256
RET_HEADS = 8
RET_DK = 128
RET_DV = 256
RET_CHUNK = 128
ROPE_BASE = 10000.0
MOBA_HEADS = 8
MOBA_DH = 128
MOBA_BLOCK = 256
MOBA_TOPK = 3
MEM_HEADS = 4
MEM_DH = 256
D_FF = 4 * D_MODEL
EPS = 1e-6
GN_EPS = 1e-5
NEG = -1e30

ROWS_S = DEC_BATCH * DEC_PAD
ROWS = SEQ + ROWS_S
TM = 1056
TN = 1024
TN_EPI = 512
LANES = 128
SEQS_PER_STEP = 2
VMEM_LIMIT = 56 * 1024 * 1024

C_RQ, C_RK, C_RV, C_RG = 0, 1024, 2048, 4096
C_MQ, C_MK, C_MV, C_CQ, C_GATE = 6144, 7168, 8192, 9216, 10240

_NT = (((1,), (1,)), ((), ()))


def _iota(shape, dim):
    return lax.broadcasted_iota(jnp.int32, shape, dim)


def _params(*sem):
    return pltpu.CompilerParams(dimension_semantics=sem, vmem_limit_bytes=VMEM_LIMIT)


def _zero_slab(width):
    return jnp.zeros((ROWS, width), BF16)


def _rmsnorm_cast(x, w, tm):
    rows, d = x.shape

    def body(x_ref, w_ref, o_ref):
        xf = x_ref[...]
        y = xf * lax.rsqrt(jnp.mean(xf * xf, axis=-1, keepdims=True) + EPS)
        o_ref[...] = (y * w_ref[...]).astype(o_ref.dtype)

    return pl.pallas_call(
        body,
        grid=(rows // tm,),
        in_specs=[pl.BlockSpec((tm, d), lambda i: (i, 0)), pl.BlockSpec((1, d), lambda i: (0, 0))],
        out_specs=pl.BlockSpec((tm, d), lambda i: (i, 0)),
        out_shape=jax.ShapeDtypeStruct((rows, d), BF16),
        compiler_params=_params("arbitrary"),
        name="rmsnorm_cast",
    )(x, w.reshape(1, d))


def _rope_tables(inv):
    tm = TM

    def body(inv_ref, c_ref, s_ref):
        row = pl.program_id(0) * tm + _iota((tm, LANES), 0)
        pos = jnp.where(row < SEQ, row, PAST_LEN + ((row - SEQ) & (DEC_PAD - 1)))
        ang = pos.astype(F32) * inv_ref[...]
        sn = jnp.sin(ang)
        c_ref[...] = jnp.cos(ang)
        s_ref[...] = jnp.where(_iota((tm, LANES), 1) < RET_DK // 2, -sn, sn)

    return pl.pallas_call(
        body,
        grid=(ROWS // tm,),
        in_specs=[pl.BlockSpec((1, LANES), lambda i: (0, 0))],
        out_specs=[pl.BlockSpec((tm, LANES), lambda i: (i, 0))] * 2,
        out_shape=[jax.ShapeDtypeStruct((ROWS, LANES), F32)] * 2,
        compiler_params=_params("arbitrary"),
        name="rope_tables",
    )(inv)


def _matmul(a, w, col0, n_cols, tm, tn, epi_fn, epi_args, out_dtypes, name):
    m, k = a.shape
    assert m % tm == 0 and n_cols % tn == 0 and col0 % tn == 0
    jb0 = col0 // tn
    n_epi = len(epi_args)

    def body(a_ref, w_ref, *rest):
        epi_refs, out_refs = rest[:n_epi], rest[n_epi:]
        acc = jnp.dot(a_ref[...], w_ref[...].astype(BF16), preferred_element_type=F32)
        for o_ref, o in zip(out_refs, epi_fn(acc, *epi_refs)):
            o_ref[...] = o.astype(o_ref.dtype)

    in_specs = [pl.BlockSpec((tm, k), lambda i, j: (i, 0)),
                pl.BlockSpec((k, tn), lambda i, j: (0, jb0 + j))]
    in_specs += [pl.BlockSpec(bs, im) for _, bs, im in epi_args]
    outs = pl.pallas_call(
        body,
        grid=(m // tm, n_cols // tn),
        in_specs=in_specs,
        out_specs=[pl.BlockSpec((tm, tn), lambda i, j: (i, j)) for _ in out_dtypes],
        out_shape=[jax.ShapeDtypeStruct((m, n_cols), dt) for dt in out_dtypes],
        compiler_params=_params("arbitrary", "arbitrary"),
        name=name,
    )(a, w, *[x for x, _, _ in epi_args])
    return outs[0] if len(outs) == 1 else outs


def _epi_plain(acc):
    return (acc,)


def _epi_sigmoid(acc):
    return (1.0 / (1.0 + jnp.exp(-acc)),)


def _epi_relu_sq(acc):
    u = jnp.maximum(acc, 0.0)
    return (u * u,)


def _epi_rope(scale, acc, c_ref, s_ref):
    c, s = c_ref[...], s_ref[...]
    pieces = []
    for hh in range(acc.shape[1] // RET_DK):
        x = acc[:, hh * RET_DK:(hh + 1) * RET_DK]
        y = x * c + pltpu.roll(x, RET_DK // 2, 1) * s
        pieces.append(y * scale if scale != 1.0 else y)
    return (jnp.concatenate(pieces, axis=1),)


def _epi_headnorm(hw, acc, w_ref):
    w = w_ref[...]
    pieces = []
    for hh in range(acc.shape[1] // hw):
        x = acc[:, hh * hw:(hh + 1) * hw]
        pieces.append(x * lax.rsqrt(jnp.mean(x * x, axis=-1, keepdims=True) + EPS) * w)
    return (jnp.concatenate(pieces, axis=1),)


def _epi_gate_first(acc, g_ref):
    return (g_ref[...] * acc,)


def _epi_gate_add(acc, g_ref, prev_ref):
    return (prev_ref[...] + g_ref[...] * acc,)


def _epi_residual(acc, x_ref):
    return (x_ref[...] + acc,)


def _gated_groupnorm(o, rg, gw):
    mu = jnp.mean(o, axis=-1, keepdims=True)
    d = o - mu
    var = jnp.mean(d * d, axis=-1, keepdims=True)
    rn = d * lax.rsqrt(var + GN_EPS) * gw
    return rg / (1.0 + jnp.exp(-rg)) * rn


def _retention_prompt(log_g, rq, rk, rv, rg, gn_w):
    C = RET_CHUNK
    nc = SEQ // C
    HP = 2

    def body(lg_ref, q_ref, k_ref, v_ref, rg_ref, gw_ref, _, o_ref, st_ref, state):
        hp, c = pl.program_id(0), pl.program_id(1)

        @pl.when(c == 0)
        def _():
            state[...] = jnp.zeros_like(state)

        diff = (_iota((C, C), 0) - _iota((C, C), 1)).astype(F32)
        ii = _iota((C, 1), 0).astype(F32)
        for hh in range(HP):
            lg = lg_ref[hp * HP + hh]
            q = q_ref[:, hh * RET_DK:(hh + 1) * RET_DK]
            k = k_ref[:, hh * RET_DK:(hh + 1) * RET_DK]
            v = v_ref[:, hh * RET_DV:(hh + 1) * RET_DV]
            dmask = jnp.where(diff >= 0, jnp.exp(jnp.maximum(diff, 0.0) * lg), 0.0)
            s = lax.dot_general(q, k, _NT, preferred_element_type=F32) * dmask
            q_decay = jnp.exp((ii + 1.0) * lg)
            k_decay = jnp.exp((C - 1.0 - ii) * lg)
            c_decay = jnp.exp(jnp.zeros((1, RET_DV), F32) + C * lg)
            s_old = state[hh]
            o = (jnp.dot(s.astype(BF16), v, preferred_element_type=F32)
                 + jnp.dot(q, s_old.astype(BF16), preferred_element_type=F32) * q_decay)
            kd_t = (k.astype(F32) * k_decay).T.astype(BF16)
            s_new = s_old * c_decay + jnp.dot(kd_t, v, preferred_element_type=F32)
            state[hh] = s_new
            cols = slice(hh * RET_DV, (hh + 1) * RET_DV)
            o_ref[:, cols] = _gated_groupnorm(o, rg_ref[:, cols], gw_ref[:, cols]).astype(o_ref.dtype)

        @pl.when(c == nc - 1)
        def _():
            st_ref[...] = state[...]

    return pl.pallas_call(
        body,
        grid=(RET_HEADS // HP, nc),
        in_specs=[
            pl.BlockSpec(memory_space=pltpu.SMEM),
            pl.BlockSpec((C, HP * RET_DK), lambda h, c: (c, h)),
            pl.BlockSpec((C, HP * RET_DK), lambda h, c: (c, h)),
            pl.BlockSpec((C, HP * RET_DV), lambda h, c: (c, h)),
            pl.BlockSpec((C, HP * RET_DV), lambda h, c: (c, h)),
            pl.BlockSpec((1, HP * RET_DV), lambda h, c: (0, h)),
            pl.BlockSpec(memory_space=pl.ANY),
        ],
        out_specs=[
            pl.BlockSpec((C, HP * RET_DV), lambda h, c: (c, h)),
            pl.BlockSpec((HP, RET_DK, RET_DV), lambda h, c: (h, 0, 0)),
        ],
        out_shape=[
            jax.ShapeDtypeStruct((ROWS, RET_HEADS * RET_DV), BF16),
            jax.ShapeDtypeStruct((RET_HEADS, RET_DK, RET_DV), F32),
        ],
        input_output_aliases={6: 0},
        scratch_shapes=[pltpu.VMEM((HP, RET_DK, RET_DV), F32)],
        compiler_params=_params("arbitrary", "arbitrary"),
        name="retention_prompt",
    )(log_g, rq, rk, rv, rg, gn_w, _zero_slab(RET_HEADS * RET_DV))


def _retention_sample(log_g, rq, rk, rv, rg, gn_w, state_ret, ret_o):
    T = DEC_SEQ
    P = DEC_PAD
    nb = SEQS_PER_STEP
    row_blk0 = SEQ // (nb * P)

    def body(lg_ref, q_ref, k_ref, v_ref, rg_ref, gw_ref, st_in_ref, _, o_ref, st_ref):
        qa, ka, va = q_ref[...].astype(F32), k_ref[...].astype(F32), v_ref[...].astype(F32)
        rga, gw = rg_ref[...], gw_ref[...]
        ri, ci = _iota((P, P), 0), _iota((P, P), 1)
        diff = (ri - ci).astype(F32)
        ii = _iota((P, 1), 0)
        iif = ii.astype(F32)
        live = ii < T
        out_rows = []
        for bb in range(nb):
            rows = slice(bb * P, (bb + 1) * P)
            out_heads = []
            for h in range(RET_HEADS):
                lg = lg_ref[h]
                q = qa[rows, h * RET_DK:(h + 1) * RET_DK]
                k = jnp.where(live, ka[rows, h * RET_DK:(h + 1) * RET_DK], 0.0)
                v = jnp.where(live, va[rows, h * RET_DV:(h + 1) * RET_DV], 0.0)
                dmask = jnp.where(diff >= 0, jnp.exp(jnp.maximum(diff, 0.0) * lg), 0.0)
                s = lax.dot_general(q, k, _NT, preferred_element_type=F32) * dmask
                q_decay = jnp.exp((iif + 1.0) * lg)
                k_decay = jnp.exp((T - 1.0 - iif) * lg)
                c_decay = jnp.exp(jnp.zeros((1, RET_DV), F32) + T * lg)
                s_old = st_in_ref[bb, h]
                o = (jnp.dot(s, v, preferred_element_type=F32)
                     + jnp.dot(q, s_old, preferred_element_type=F32) * q_decay)
                upd = lax.dot_general(k * k_decay, v, (((0,), (0,)), ((), ())),
                                      preferred_element_type=F32)
                st_ref[bb, h] = s_old * c_decay + upd
                out_heads.append(_gated_groupnorm(
                    o, rga[rows, h * RET_DV:(h + 1) * RET_DV], gw[:, h * RET_DV:(h + 1) * RET_DV]))
            out_rows.append(jnp.concatenate(out_heads, axis=1))
        o_ref[...] = jnp.concatenate(out_rows, axis=0).astype(o_ref.dtype)

    wq, wv = RET_HEADS * RET_DK, RET_HEADS * RET_DV
    return pl.pallas_call(
        body,
        grid=(DEC_BATCH // nb,),
        in_specs=[
            pl.BlockSpec(memory_space=pltpu.SMEM),
            pl.BlockSpec((nb * P, wq), lambda g: (row_blk0 + g, 0)),
            pl.BlockSpec((nb * P, wq), lambda g: (row_blk0 + g, 0)),
            pl.BlockSpec((nb * P, wv), lambda g: (row_blk0 + g, 0)),
            pl.BlockSpec((nb * P, wv), lambda g: (row_blk0 + g, 0)),
            pl.BlockSpec((1, wv), lambda g: (0, 0)),
            pl.BlockSpec((nb, RET_HEADS, RET_DK, RET_DV), lambda g: (g, 0, 0, 0)),
            pl.BlockSpec(memory_space=pl.ANY),
        ],
        out_specs=[
            pl.BlockSpec((nb * P, wv), lambda g: (row_blk0 + g, 0)),
            pl.BlockSpec((nb, RET_HEADS, RET_DK, RET_DV), lambda g: (g, 0, 0, 0)),
        ],
        out_shape=[
            jax.ShapeDtypeStruct((ROWS, wv), BF16),
            jax.ShapeDtypeStruct((DEC_BATCH, RET_HEADS, RET_DK, RET_DV), F32),
        ],
        input_output_aliases={7: 0},
        compiler_params=_params("arbitrary"),
        name="retention_sample",
    )(log_g, rq, rk, rv, rg, gn_w, state_ret, ret_o)


def _moba_prompt(mq, mk, mv):
    BL = MOBA_BLOCK
    nblk = SEQ // BL
    TQ = 2 * BL
    scale = MOBA_DH ** -0.5

    def body(q_ref, k_ref, v_ref, _, o_ref, kaug, vt, kmean):
        ti = pl.program_id(1)

        @pl.when(ti == 0)
        def _():
            kmean[...] = jnp.zeros_like(kmean)
            lane = _iota((BL, LANES), 1)

            def prep(jb, carry):
                r0 = pl.multiple_of(jb * BL, BL)
                kb = k_ref[pl.ds(r0, BL), :]
                kmean[pl.ds(jb, 1), :] = jnp.mean(kb, axis=0, keepdims=True)
                kaug[pl.ds(r0, BL), 0:MOBA_DH] = kb.astype(BF16)
                kaug[pl.ds(r0, BL), MOBA_DH:2 * MOBA_DH] = (lane == jb).astype(BF16)
                vt[:, pl.ds(r0, BL)] = v_ref[pl.ds(r0, BL), :].T.astype(BF16)
                return carry

            lax.fori_loop(0, nblk, prep, 0)

        q = q_ref[...]
        gt = lax.dot_general(kmean[0:nblk, :], q, _NT, precision=lax.Precision.HIGHEST,
                             preferred_element_type=F32)
        row = _iota((nblk, TQ), 0)
        own = 2 * ti + jnp.where(_iota((nblk, TQ), 1) >= BL, 1, 0)
        cnt = jnp.zeros((nblk, TQ), jnp.int32)
        for blk in range(nblk):
            g_blk = gt[blk:blk + 1, :]
            beats = (g_blk > gt) | ((g_blk == gt) & (row > blk))
            cnt = cnt + jnp.where(beats & (own > blk), 1, 0)
        keep = ((cnt < MOBA_TOPK) & (row < own)) | (row == own)
        bias_t = jnp.where(keep, 0.0, NEG)
        bias_t = jnp.concatenate([bias_t, jnp.zeros((LANES - nblk, TQ), F32)], axis=0)
        qa = jnp.concatenate([(q * scale).astype(BF16), bias_t.T.astype(BF16)], axis=1)

        def scores(g):
            c0 = pl.multiple_of(g * TQ, TQ)
            return lax.dot_general(kaug[pl.ds(c0, TQ), :], qa, _NT, preferred_element_type=F32)

        r0 = pl.multiple_of(ti * TQ, TQ)
        s = jnp.where(_iota((TQ, TQ), 0) <= _iota((TQ, TQ), 1), scores(ti), NEG)
        m = jnp.max(s, axis=0, keepdims=True)
        p = jnp.exp(s - m)
        l = jnp.sum(p, axis=0, keepdims=True)
        acc = jnp.dot(vt[:, pl.ds(r0, TQ)], p.astype(BF16), preferred_element_type=F32)

        def step(g, carry):
            m, l, acc, s = carry
            s_next = scores(g + 1)
            c0 = pl.multiple_of(g * TQ, TQ)
            m_new = jnp.maximum(m, jnp.max(s, axis=0, keepdims=True))
            alpha = jnp.exp(m - m_new)
            p = jnp.exp(s - m_new)
            l = l * alpha + jnp.sum(p, axis=0, keepdims=True)
            acc = acc * alpha + jnp.dot(vt[:, pl.ds(c0, TQ)], p.astype(BF16),
                                        preferred_element_type=F32)
            return m_new, l, acc, s_next

        m, l, acc, _ = lax.fori_loop(0, ti, step, (m, l, acc, scores(0)))
        o_ref[...] = (acc / l).T.astype(o_ref.dtype)

    return pl.pallas_call(
        body,
        grid=(MOBA_HEADS, SEQ // TQ),
        in_specs=[
            pl.BlockSpec((TQ, MOBA_DH), lambda h, i: (i, h)),
            pl.BlockSpec((SEQ, MOBA_DH), lambda h, i: (0, h)),
            pl.BlockSpec((SEQ, MOBA_DH), lambda h, i: (0, h)),
            pl.BlockSpec(memory_space=pl.ANY),
        ],
        out_specs=pl.BlockSpec((TQ, MOBA_DH), lambda h, i: (i, h)),
        out_shape=jax.ShapeDtypeStruct((ROWS, MOBA_HEADS * MOBA_DH), BF16),
        input_output_aliases={3: 0},
        scratch_shapes=[
            pltpu.VMEM((SEQ, 2 * MOBA_DH), BF16),
            pltpu.VMEM((MOBA_DH, SEQ), BF16),
            pltpu.VMEM((LANES, MOBA_DH), F32),
        ],
        compiler_params=_params("arbitrary", "arbitrary"),
        name="moba_prompt",
    )(mq, mk, mv, _zero_slab(MOBA_HEADS * MOBA_DH))


def _sample_block_means(cache_k, page_table):
    ppb = MOBA_BLOCK // PAGE_SIZE
    nblk = PAST_LEN // MOBA_BLOCK
    bps = 8
    pps = bps * ppb

    def body(pt_ref, *refs):
        pages, o_ref = refs[:pps], refs[pps]
        g = pl.program_id(1)
        for kb in range(bps):
            tot = jnp.sum(pages[ppb * kb][0], axis=0)
            for pg in range(1, ppb):
                tot = tot + jnp.sum(pages[ppb * kb + pg][0], axis=0)
            o_ref[0, g * bps + kb] = tot * (1.0 / MOBA_BLOCK)

    def page_spec(n):
        return pl.BlockSpec((1, PAGE_SIZE, MOBA_HEADS, MOBA_DH),
                            lambda b, g, pt: (pt[b, g * pps + n], 0, 0, 0))

    grid_spec = pltpu.PrefetchScalarGridSpec(
        num_scalar_prefetch=1,
        grid=(DEC_BATCH, nblk // bps),
        in_specs=[page_spec(n) for n in range(pps)],
        out_specs=pl.BlockSpec((1, nblk, MOBA_HEADS, MOBA_DH), lambda b, g, pt: (b, 0, 0, 0)),
    )
    return pl.pallas_call(
        body,
        grid_spec=grid_spec,
        out_shape=jax.ShapeDtypeStruct((DEC_BATCH, nblk, MOBA_HEADS, MOBA_DH), F32),
        compiler_params=_params("arbitrary", "arbitrary"),
        name="sample_block_means",
    )(page_table, *([cache_k] * pps))


def _sample_block_select(mq, kmean):
    nblk = PAST_LEN // MOBA_BLOCK
    P = DEC_PAD
    row_blk0 = SEQ // P

    def body(q_ref, km_ref, o_ref):
        q = q_ref[...]
        lane = _iota((P, LANES), 1)
        for h in range(MOBA_HEADS):
            cols = slice(h * MOBA_DH, (h + 1) * MOBA_DH)
            kmh = jnp.concatenate([km_ref[0, :, h, :], jnp.zeros((LANES - nblk, MOBA_DH), F32)], axis=0)
            gate = lax.dot_general(q[:, cols], kmh, _NT, precision=lax.Precision.HIGHEST,
                                   preferred_element_type=F32)
            g = jnp.where(lane < nblk, gate, -jnp.inf)
            ids = []
            for _ in range(MOBA_TOPK):
                mx = jnp.max(g, axis=-1, keepdims=True)
                first = jnp.min(jnp.where(g == mx, lane, 1 << 20), axis=-1, keepdims=True)
                ids.append(first)
                g = jnp.where(lane == first, -jnp.inf, g)
            o_ref[0, h] = jnp.where(lane == 0, ids[0], jnp.where(lane == 1, ids[1], ids[2]))

    return pl.pallas_call(
        body,
        grid=(DEC_BATCH,),
        in_specs=[
            pl.BlockSpec((P, MOBA_HEADS * MOBA_DH), lambda b: (row_blk0 + b, 0)),
            pl.BlockSpec((1, nblk, MOBA_HEADS, MOBA_DH), lambda b: (b, 0, 0, 0)),
        ],
        out_specs=pl.BlockSpec((1, MOBA_HEADS, P, LANES), lambda b: (b, 0, 0, 0)),
        out_shape=jax.ShapeDtypeStruct((DEC_BATCH, MOBA_HEADS, P, LANES), jnp.int32),
        compiler_params=_params("arbitrary"),
        name="sample_block_select",
    )(mq, kmean)


def _sample_attend(mq, mk, mv, cache_k, cache_v, page_table, blk_ids, moba_o):
    ppb = MOBA_BLOCK // PAGE_SIZE
    n_sel = MOBA_TOPK * MOBA_BLOCK
    nb = SEQS_PER_STEP
    n_q = nb * DEC_SEQ
    n_copies = n_q * MOBA_TOPK * ppb
    n_steps = (DEC_BATCH // nb) * MOBA_HEADS
    scale = MOBA_DH ** -0.5
    P = DEC_PAD
    row_blk0 = SEQ // (nb * P)

    def body(pt_ref, ids_ref, q_ref, kn_ref, vn_ref, ck_ref, cv_ref, _, o_ref, kbuf, vbuf, sem):
        step = pl.program_id(0) * MOBA_HEADS + pl.program_id(1)

        def copies(st, slot):
            g, h = st // MOBA_HEADS, st % MOBA_HEADS
            out = []
            for qi in range(n_q):
                b, t = g * nb + qi // DEC_SEQ, qi % DEC_SEQ
                for sl in range(MOBA_TOPK):
                    blk = ids_ref[((b * DEC_SEQ + t) * MOBA_HEADS + h) * MOBA_TOPK + sl]
                    for pg in range(ppb):
                        phys = pt_ref[b, blk * ppb + pg]
                        n = (qi * MOBA_TOPK + sl) * ppb + pg
                        rows = pl.ds((sl * ppb + pg) * PAGE_SIZE, PAGE_SIZE)
                        for src, dst, which in ((ck_ref, kbuf, 0), (cv_ref, vbuf, 1)):
                            out.append(pltpu.make_async_copy(src.at[phys, :, h, :], dst.at[slot, qi, rows, :],
                                                             sem.at[slot, which, n]))
            return out

        slot = step % 2

        @pl.when(step == 0)
        def _():
            for n, cp in enumerate(copies(step, 0)):
                cp.start(priority=n % 2)

        @pl.when(step + 1 < n_steps)
        def _():
            for n, cp in enumerate(copies(step + 1, 1 - slot)):
                cp.start(priority=n % 2)

        for cp in copies(step, slot):
            cp.wait()

        q, kn, vn = q_ref[...], kn_ref[...], vn_ref[...]
        rown = _iota((P, 1), 0)
        out_rows = []
        for bb in range(nb):
            knb, vnb = kn[bb * P:(bb + 1) * P], vn[bb * P:(bb + 1) * P]
            res = jnp.zeros((P, MOBA_DH), F32)
            for t in range(DEC_SEQ):
                qi = bb * DEC_SEQ + t
                qt = q[bb * P + t:bb * P + t + 1, :] * scale
                s = jnp.sum(kbuf[slot, qi] * qt, axis=-1, keepdims=True)
                sn = jnp.where(rown <= t, jnp.sum(knb * qt, axis=-1, keepdims=True), NEG)
                m = jnp.maximum(jnp.max(s, axis=0, keepdims=True), jnp.max(sn, axis=0, keepdims=True))
                p, pn = jnp.exp(s - m), jnp.exp(sn - m)
                l = jnp.sum(p, axis=0, keepdims=True) + jnp.sum(pn, axis=0, keepdims=True)
                o = (jnp.sum(p * vbuf[slot, qi], axis=0, keepdims=True)
                     + jnp.sum(pn * vnb, axis=0, keepdims=True))
                res = jnp.where(rown == t, o / l, res)
            out_rows.append(res)
        o_ref[...] = jnp.concatenate(out_rows, axis=0).astype(o_ref.dtype)

    head_blk = pl.BlockSpec((nb * P, MOBA_DH), lambda g, h, pt, ids: (row_blk0 + g, h))
    any_spec = pl.BlockSpec(memory_space=pl.ANY)
    grid_spec = pltpu.PrefetchScalarGridSpec(
        num_scalar_prefetch=2,
        grid=(DEC_BATCH // nb, MOBA_HEADS),
        in_specs=[head_blk, head_blk, head_blk, any_spec, any_spec, any_spec],
        out_specs=head_blk,
        scratch_shapes=[
            pltpu.VMEM((2, n_q, n_sel, MOBA_DH), F32),
            pltpu.VMEM((2, n_q, n_sel, MOBA_DH), F32),
            pltpu.SemaphoreType.DMA((2, 2, n_copies)),
        ],
    )
    return pl.pallas_call(
        body,
        grid_spec=grid_spec,
        out_shape=jax.ShapeDtypeStruct((ROWS, MOBA_HEADS * MOBA_DH), BF16),
        input_output_aliases={7: 0},
        compiler_params=_params("arbitrary", "arbitrary"),
        name="sample_attend",
    )(page_table, blk_ids, mq, mk, mv, cache_k, cache_v, moba_o)


def _softmax_attend(q, k, v):
    s = lax.dot_general(q, k, _NT, preferred_element_type=F32)
    m = jnp.max(s, axis=-1, keepdims=True)
    p = jnp.exp(s - m)
    l = jnp.sum(p, axis=-1, keepdims=True)
    return jnp.dot(p.astype(v.dtype), v, preferred_element_type=F32) / l


def _mem_attend_prompt(cq, mem_k, mem_v):
    tq = 512
    scale = MEM_DH ** -0.5

    def body(q_ref, k_ref, v_ref, _, o_ref):
        q = (q_ref[...] * scale).astype(BF16)
        o_ref[...] = _softmax_attend(q, k_ref[...].astype(BF16), v_ref[...].astype(BF16)).astype(o_ref.dtype)

    kv_spec = pl.BlockSpec((N_MEM, MEM_DH), lambda r, h: (0, h))
    return pl.pallas_call(
        body,
        grid=(SEQ // tq, MEM_HEADS),
        in_specs=[pl.BlockSpec((tq, MEM_DH), lambda r, h: (r, h)), kv_spec, kv_spec,
                  pl.BlockSpec(memory_space=pl.ANY)],
        out_specs=pl.BlockSpec((tq, MEM_DH), lambda r, h: (r, h)),
        out_shape=jax.ShapeDtypeStruct((ROWS, MEM_HEADS * MEM_DH), BF16),
        input_output_aliases={3: 0},
        compiler_params=_params("arbitrary", "arbitrary"),
        name="mem_attend_prompt",
    )(cq, mem_k, mem_v, _zero_slab(MEM_HEADS * MEM_DH))


def _mem_attend_sample(cq, mem_k, mem_v, mem_o):
    nb = SEQS_PER_STEP
    P = DEC_PAD
    w = MEM_HEADS * MEM_DH
    scale = MEM_DH ** -0.5
    row_blk0 = SEQ // (nb * P)

    def body(q_ref, k_ref, v_ref, _, o_ref):
        qa = q_ref[...] * scale
        out_rows = []
        for bb in range(nb):
            heads = []
            for h in range(MEM_HEADS):
                cols = slice(h * MEM_DH, (h + 1) * MEM_DH)
                heads.append(_softmax_attend(qa[bb * P:(bb + 1) * P, cols], k_ref[bb, :, h, :],
                                             v_ref[bb, :, h, :]))
            out_rows.append(jnp.concatenate(heads, axis=1))
        o_ref[...] = jnp.concatenate(out_rows, axis=0).astype(o_ref.dtype)

    kv_spec = pl.BlockSpec((nb, N_MEM, MEM_HEADS, MEM_DH), lambda g: (g, 0, 0, 0))
    row_spec = pl.BlockSpec((nb * P, w), lambda g: (row_blk0 + g, 0))
    return pl.pallas_call(
        body,
        grid=(DEC_BATCH // nb,),
        in_specs=[row_spec, kv_spec, kv_spec, pl.BlockSpec(memory_space=pl.ANY)],
        out_specs=row_spec,
        out_shape=jax.ShapeDtypeStruct((ROWS, w), BF16),
        input_output_aliases={3: 0},
        compiler_params=_params("arbitrary"),
        name="mem_attend_sample",
    )(cq, mem_k, mem_v, mem_o)


def kernel(x_prompt, x_sample, cache_k, cache_v, cache_mem_k, cache_mem_v, state_ret, page_table, mem_prompt, attn_norm_w, w_in, moba_q_norm, moba_k_norm, mem_q_norm, mem_k_norm, mem_norm_w, w_mem_kv, ret_gn_w, w_ret_o, w_moba_o, w_mem_o, w_out, mlp_norm_w, w_up, w_down):
    hd_mem = MEM_HEADS * MEM_DH

    xs = jnp.pad(x_sample, ((0, 0), (0, DEC_PAD - DEC_SEQ), (0, 0))).reshape(ROWS_S, D_MODEL)
    x = jnp.concatenate([x_prompt[0], xs], axis=0)

    half = RET_DK // 2
    inv = ROPE_BASE ** (-jnp.arange(half, dtype=F32) * (2.0 / RET_DK))
    inv = jnp.concatenate([inv, inv]).reshape(1, LANES)
    log_g = jnp.log1p(-jnp.exp2(-5.0 - jnp.arange(RET_HEADS, dtype=F32)))
    cos_t, sin_t = _rope_tables(inv)

    xn = _rmsnorm_cast(x, attn_norm_w, 384)

    rope_args = [(cos_t, (TM, LANES), lambda i, j: (i, 0)), (sin_t, (TM, LANES), lambda i, j: (i, 0))]

    def norm_arg(wv):
        return [(wv.reshape(1, -1), (1, wv.shape[0]), lambda i, j: (0, 0))]

    def proj(col0, n_cols, epi, args, dt, name):
        return _matmul(xn, w_in, col0, n_cols, TM, TN, epi, args, [dt], name)

    rq = proj(C_RQ, 1024, functools.partial(_epi_rope, 1.0), rope_args, BF16, "proj_rq")
    rk = proj(C_RK, 1024, functools.partial(_epi_rope, RET_DK ** -0.5), rope_args, BF16, "proj_rk")
    rv = proj(C_RV, 2048, _epi_plain, [], BF16, "proj_rv")
    rg = proj(C_RG, 2048, _epi_plain, [], F32, "proj_rg")
    mq = proj(C_MQ, 1024, functools.partial(_epi_headnorm, MOBA_DH), norm_arg(moba_q_norm), F32, "proj_mq")
    mk = proj(C_MK, 1024, functools.partial(_epi_headnorm, MOBA_DH), norm_arg(moba_k_norm), F32, "proj_mk")
    mv = proj(C_MV, 1024, _epi_plain, [], F32, "proj_mv")
    cq = proj(C_CQ, 1024, functools.partial(_epi_headnorm, MEM_DH), norm_arg(mem_q_norm), F32, "proj_cq")
    gates = proj(C_GATE, 3 * D_MODEL, _epi_sigmoid, [], F32, "proj_gates")

    mn = _rmsnorm_cast(mem_prompt[0], mem_norm_w, N_MEM)
    mem_k_p = _matmul(mn, w_mem_kv, 0, hd_mem, N_MEM, TN, functools.partial(_epi_headnorm, MEM_DH),
                      norm_arg(mem_k_norm), [F32], "mem_k")
    mem_v_p = _matmul(mn, w_mem_kv, hd_mem, hd_mem, N_MEM, TN, _epi_plain, [], [F32], "mem_v")

    gn_w = ret_gn_w.reshape(1, -1)
    ret_o, ret_state_p = _retention_prompt(log_g, rq, rk, rv, rg, gn_w)
    ret_o, ret_state_s = _retention_sample(log_g, rq, rk, rv, rg, gn_w, state_ret, ret_o)

    moba_o = _moba_prompt(mq, mk, mv)
    kmean_s = _sample_block_means(cache_k, page_table)
    ids = _sample_block_select(mq, kmean_s)
    ids = ids[:, :, :DEC_SEQ, :MOBA_TOPK].transpose(0, 2, 1, 3).reshape(-1)
    moba_o = _sample_attend(mq, mk, mv, cache_k, cache_v, page_table, ids, moba_o)

    mem_o = _mem_attend_prompt(cq, mem_k_p, mem_v_p)
    mem_o = _mem_attend_sample(cq, cache_mem_k, cache_mem_v, mem_o)

    tn = TN_EPI

    def gate_arg(branch):
        off = branch * D_MODEL // tn
        return (gates, (TM, tn), lambda i, j: (i, off + j))

    def tile_arg(arr):
        return (arr, (TM, tn), lambda i, j: (i, j))

    merged = _matmul(ret_o, w_ret_o, 0, D_MODEL, TM, tn, _epi_gate_first, [gate_arg(0)], [F32], "branch_ret")
    merged = _matmul(moba_o, w_moba_o, 0, D_MODEL, TM, tn, _epi_gate_add,
                     [gate_arg(1), tile_arg(merged)], [F32], "branch_moba")
    merged = _matmul(mem_o, w_mem_o, 0, D_MODEL, TM, tn, _epi_gate_add,
                     [gate_arg(2), tile_arg(merged)], [BF16], "branch_mem")
    hres = _matmul(merged, w_out, 0, D_MODEL, TM, tn, _epi_residual, [tile_arg(x)], [F32], "out_proj")
    hn = _rmsnorm_cast(hres, mlp_norm_w, 384)
    u2 = _matmul(hn, w_up, 0, D_FF, TM, TN, _epi_relu_sq, [], [BF16], "mlp_up")
    tm_down = TM // 2
    y = _matmul(u2, w_down.astype(BF16), 0, D_MODEL, tm_down, tn, _epi_residual,
                [(hres, (tm_down, tn), lambda i, j: (i, j))], [F32], "mlp_down")

    def sample_rows(a):
        return a.reshape(DEC_BATCH, DEC_PAD, *a.shape[1:])[:, :DEC_SEQ]

    shape4 = (MOBA_HEADS, MOBA_DH)
    return (
        y[:SEQ][None],
        sample_rows(y[SEQ:]),
        mk[:SEQ].reshape(1, SEQ, *shape4),
        mv[:SEQ].reshape(1, SEQ, *shape4),
        ret_state_p[None],
        mem_k_p.reshape(1, N_MEM, MEM_HEADS, MEM_DH),
        mem_v_p.reshape(1, N_MEM, MEM_HEADS, MEM_DH),
        sample_rows(mk[SEQ:]).reshape(DEC_BATCH, DEC_SEQ, *shape4),
        sample_rows(mv[SEQ:]).reshape(DEC_BATCH, DEC_SEQ, *shape4),
        ret_state_s,
    )
```

```python
import functools

import jax
import jax.numpy as jnp
from jax import lax
from jax.experimental import pallas as pl
from jax.experimental.pallas import tpu as pltpu

F32 = jnp.float32
BF16 = jnp.bfloat16

D_MODEL = 2048
SEQ = 8192
DEC_BATCH = 32
DEC_SEQ = 4
DEC_PAD = 8
PAST_LEN = 16384
PAGE_SIZE = 128
N_MEM = 256
RET_HEADS = 8
RET_DK = 128
RET_DV = 256
RET_CHUNK = 128
ROPE_BASE = 10000.0
MOBA_HEADS = 8
MOBA_DH = 128
MOBA_BLOCK = 256
MOBA_TOPK = 3
MEM_HEADS = 4
MEM_DH = 256
D_FF = 4 * D_MODEL
EPS = 1e-6
GN_EPS = 1e-5
NEG = -1e30
LOG2_E = 1.4426950408889634

ROWS_S = DEC_BATCH * DEC_PAD
ROWS = SEQ + ROWS_S
TM = 1056
TN = 1024
TN_EPI = 512
LANES = 128
SEQS_PER_STEP = 2
VMEM_LIMIT = 56 * 1024 * 1024

C_RQ, C_RK, C_RV, C_RG = 0, 1024, 2048, 4096
C_MQ, C_MK, C_MV, C_CQ, C_GATE = 6144, 7168, 8192, 9216, 10240

_NT = (((1,), (1,)), ((), ()))


def _iota(shape, dim):
    return lax.broadcasted_iota(jnp.int32, shape, dim)


def _params(*sem):
    return pltpu.CompilerParams(dimension_semantics=sem, vmem_limit_bytes=VMEM_LIMIT)


def _zero_slab(width):
    return jnp.zeros((ROWS, width), BF16)


def _rmsnorm_cast(x, w, tm):
    rows, d = x.shape

    def body(x_ref, w_ref, o_ref):
        xf = x_ref[...]
        y = xf * lax.rsqrt(jnp.mean(xf * xf, axis=-1, keepdims=True) + EPS)
        o_ref[...] = (y * w_ref[...]).astype(o_ref.dtype)

    return pl.pallas_call(
        body,
        grid=(rows // tm,),
        in_specs=[pl.BlockSpec((tm, d), lambda i: (i, 0)), pl.BlockSpec((1, d), lambda i: (0, 0))],
        out_specs=pl.BlockSpec((tm, d), lambda i: (i, 0)),
        out_shape=jax.ShapeDtypeStruct((rows, d), BF16),
        compiler_params=_params("arbitrary"),
        name="rmsnorm_cast",
    )(x, w.reshape(1, d))


def _rope_tables(inv):
    tm = TM

    def body(inv_ref, c_ref, s_ref):
        row = pl.program_id(0) * tm + _iota((tm, LANES), 0)
        pos = jnp.where(row < SEQ, row, PAST_LEN + ((row - SEQ) & (DEC_PAD - 1)))
        ang = pos.astype(F32) * inv_ref[...]
        sn = jnp.sin(ang)
        c_ref[...] = jnp.cos(ang)
        s_ref[...] = jnp.where(_iota((tm, LANES), 1) < RET_DK // 2, -sn, sn)

    return pl.pallas_call(
        body,
        grid=(ROWS // tm,),
        in_specs=[pl.BlockSpec((1, LANES), lambda i: (0, 0))],
        out_specs=[pl.BlockSpec((tm, LANES), lambda i: (i, 0))] * 2,
        out_shape=[jax.ShapeDtypeStruct((ROWS, LANES), F32)] * 2,
        compiler_params=_params("arbitrary"),
        name="rope_tables",
    )(inv)


def _matmul(a, w, col0, n_cols, tm, tn, epi_fn, epi_args, out_dtypes, name):
    m, k = a.shape
    assert m % tm == 0 and n_cols % tn == 0 and col0 % tn == 0
    jb0 = col0 // tn
    n_epi = len(epi_args)

    def body(a_ref, w_ref, *rest):
        epi_refs, out_refs = rest[:n_epi], rest[n_epi:]
        acc = jnp.dot(a_ref[...], w_ref[...].astype(BF16), preferred_element_type=F32)
        for o_ref, o in zip(out_refs, epi_fn(acc, *epi_refs)):
            o_ref[...] = o.astype(o_ref.dtype)

    in_specs = [pl.BlockSpec((tm, k), lambda i, j: (i, 0)),
                pl.BlockSpec((k, tn), lambda i, j: (0, jb0 + j))]
    in_specs += [pl.BlockSpec(bs, im) for _, bs, im in epi_args]
    outs = pl.pallas_call(
        body,
        grid=(m // tm, n_cols // tn),
        in_specs=in_specs,
        out_specs=[pl.BlockSpec((tm, tn), lambda i, j: (i, j)) for _ in out_dtypes],
        out_shape=[jax.ShapeDtypeStruct((m, n_cols), dt) for dt in out_dtypes],
        compiler_params=_params("arbitrary", "arbitrary"),
        name=name,
    )(a, w, *[x for x, _, _ in epi_args])
    return outs[0] if len(outs) == 1 else outs


def _epi_plain(acc):
    return (acc,)


def _epi_sigmoid(acc):
    return (1.0 / (1.0 + jnp.exp(-acc)),)


def _epi_relu_sq(acc):
    u = jnp.maximum(acc, 0.0)
    return (u * u,)


def _epi_rope(scale, acc, c_ref, s_ref):
    c, s = c_ref[...], s_ref[...]
    pieces = []
    for hh in range(acc.shape[1] // RET_DK):
        x = acc[:, hh * RET_DK:(hh + 1) * RET_DK]
        y = x * c + pltpu.roll(x, RET_DK // 2, 1) * s
        pieces.append(y * scale if scale != 1.0 else y)
    return (jnp.concatenate(pieces, axis=1),)


def _epi_headnorm(hw, acc, w_ref):
    w = w_ref[...]
    pieces = []
    for hh in range(acc.shape[1] // hw):
        x = acc[:, hh * hw:(hh + 1) * hw]
        pieces.append(x * lax.rsqrt(jnp.mean(x * x, axis=-1, keepdims=True) + EPS) * w)
    return (jnp.concatenate(pieces, axis=1),)


def _epi_residual(acc, x_ref):
    return (x_ref[...] + acc,)


def _gated_branch_merge(branches, weights, gates):
    n_b = len(branches)
    tm, tn = TM, TN_EPI

    def body(*refs):
        a_refs, w_refs, g_refs, o_ref = refs[:n_b], refs[n_b:2 * n_b], refs[2 * n_b:3 * n_b], refs[3 * n_b]
        acc = None
        for a_ref, w_ref, g_ref in zip(a_refs, w_refs, g_refs):
            term = g_ref[...].astype(F32) * jnp.dot(a_ref[...], w_ref[...], preferred_element_type=F32)
            acc = term if acc is None else acc + term
        o_ref[...] = acc.astype(o_ref.dtype)

    in_specs = [pl.BlockSpec((tm, a.shape[1]), lambda i, j: (i, 0)) for a in branches]
    in_specs += [pl.BlockSpec((w.shape[0], tn), lambda i, j: (0, j)) for w in weights]
    in_specs += [pl.BlockSpec((tm, tn), lambda i, j, b=b: (i, b * (D_MODEL // tn) + j)) for b in range(n_b)]
    return pl.pallas_call(
        body,
        grid=(ROWS // tm, D_MODEL // tn),
        in_specs=in_specs,
        out_specs=pl.BlockSpec((tm, tn), lambda i, j: (i, j)),
        out_shape=jax.ShapeDtypeStruct((ROWS, D_MODEL), BF16),
        compiler_params=_params("arbitrary", "arbitrary"),
        name="branch_merge",
    )(*branches, *weights, *([gates] * n_b))


def _gated_groupnorm(o, rg, gw):
    mu = jnp.mean(o, axis=-1, keepdims=True)
    d = o - mu
    var = jnp.mean(d * d, axis=-1, keepdims=True)
    rn = d * lax.rsqrt(var + GN_EPS) * gw
    return rg / (1.0 + jnp.exp(-rg)) * rn


def _retention_prompt(log_g, rq, rk, rv, rg, gn_w):
    C = RET_CHUNK
    nc = SEQ // C
    HP = 4

    def body(lg_ref, q_ref, k_ref, v_ref, rg_ref, gw_ref, _, o_ref, st_ref, state):
        hp, c = pl.program_id(0), pl.program_id(1)

        @pl.when(c == 0)
        def _():
            state[...] = jnp.zeros_like(state)

        diff = (_iota((C, C), 0) - _iota((C, C), 1)).astype(F32)
        ii = _iota((C, 1), 0).astype(F32)
        for hh in range(HP):
            lg = lg_ref[hp * HP + hh]
            q = q_ref[:, hh * RET_DK:(hh + 1) * RET_DK]
            k = k_ref[:, hh * RET_DK:(hh + 1) * RET_DK]
            v = v_ref[:, hh * RET_DV:(hh + 1) * RET_DV]
            dmask = jnp.where(diff >= 0, jnp.exp(jnp.maximum(diff, 0.0) * lg), 0.0)
            s = lax.dot_general(q, k, _NT, preferred_element_type=F32) * dmask
            q_decay = jnp.exp((ii + 1.0) * lg)
            k_decay = jnp.exp((C - 1.0 - ii) * lg)
            c_decay = jnp.exp(jnp.zeros((1, RET_DV), F32) + C * lg)
            s_old = state[hh]
            o = (jnp.dot(s.astype(BF16), v, preferred_element_type=F32)
                 + jnp.dot(q, s_old.astype(BF16), preferred_element_type=F32) * q_decay)
            kd_t = (k.astype(F32) * k_decay).T.astype(BF16)
            s_new = s_old * c_decay + jnp.dot(kd_t, v, preferred_element_type=F32)
            state[hh] = s_new
            cols = slice(hh * RET_DV, (hh + 1) * RET_DV)
            o_ref[:, cols] = _gated_groupnorm(o, rg_ref[:, cols], gw_ref[:, cols]).astype(o_ref.dtype)

        @pl.when(c == nc - 1)
        def _():
            st_ref[...] = state[...]

    return pl.pallas_call(
        body,
        grid=(RET_HEADS // HP, nc),
        in_specs=[
            pl.BlockSpec(memory_space=pltpu.SMEM),
            pl.BlockSpec((C, HP * RET_DK), lambda h, c: (c, h)),
            pl.BlockSpec((C, HP * RET_DK), lambda h, c: (c, h)),
            pl.BlockSpec((C, HP * RET_DV), lambda h, c: (c, h)),
            pl.BlockSpec((C, HP * RET_DV), lambda h, c: (c, h)),
            pl.BlockSpec((1, HP * RET_DV), lambda h, c: (0, h)),
            pl.BlockSpec(memory_space=pl.ANY),
        ],
        out_specs=[
            pl.BlockSpec((C, HP * RET_DV), lambda h, c: (c, h)),
            pl.BlockSpec((HP, RET_DK, RET_DV), lambda h, c: (h, 0, 0)),
        ],
        out_shape=[
            jax.ShapeDtypeStruct((ROWS, RET_HEADS * RET_DV), BF16),
            jax.ShapeDtypeStruct((RET_HEADS, RET_DK, RET_DV), F32),
        ],
        input_output_aliases={6: 0},
        scratch_shapes=[pltpu.VMEM((HP, RET_DK, RET_DV), F32)],
        compiler_params=_params("arbitrary", "arbitrary"),
        name="retention_prompt",
    )(log_g, rq, rk, rv, rg, gn_w, _zero_slab(RET_HEADS * RET_DV))


def _retention_sample(log_g, rq, rk, rv, rg, gn_w, state_ret, ret_o):
    T = DEC_SEQ
    P = DEC_PAD
    nb = SEQS_PER_STEP
    row_blk0 = SEQ // (nb * P)

    def body(lg_ref, q_ref, k_ref, v_ref, rg_ref, gw_ref, st_in_ref, _, o_ref, st_ref):
        qa, ka, va = q_ref[...].astype(F32), k_ref[...].astype(F32), v_ref[...].astype(F32)
        rga, gw = rg_ref[...], gw_ref[...]
        ri, ci = _iota((P, P), 0), _iota((P, P), 1)
        diff = (ri - ci).astype(F32)
        ii = _iota((P, 1), 0)
        iif = ii.astype(F32)
        live = ii < T
        out_rows = []
        for bb in range(nb):
            rows = slice(bb * P, (bb + 1) * P)
            out_heads = []
            for h in range(RET_HEADS):
                lg = lg_ref[h]
                q = qa[rows, h * RET_DK:(h + 1) * RET_DK]
                k = jnp.where(live, ka[rows, h * RET_DK:(h + 1) * RET_DK], 0.0)
                v = jnp.where(live, va[rows, h * RET_DV:(h + 1) * RET_DV], 0.0)
                dmask = jnp.where(diff >= 0, jnp.exp(jnp.maximum(diff, 0.0) * lg), 0.0)
                s = lax.dot_general(q, k, _NT, preferred_element_type=F32) * dmask
                q_decay = jnp.exp((iif + 1.0) * lg)
                k_decay = jnp.exp((T - 1.0 - iif) * lg)
                c_decay = jnp.exp(jnp.zeros((1, RET_DV), F32) + T * lg)
                s_old = st_in_ref[bb, h]
                o = (jnp.dot(s, v, preferred_element_type=F32)
                     + jnp.dot(q, s_old, preferred_element_type=F32) * q_decay)
                upd = lax.dot_general(k * k_decay, v, (((0,), (0,)), ((), ())),
                                      preferred_element_type=F32)
                st_ref[bb, h] = s_old * c_decay + upd
                out_heads.append(_gated_groupnorm(
                    o, rga[rows, h * RET_DV:(h + 1) * RET_DV], gw[:, h * RET_DV:(h + 1) * RET_DV]))
            out_rows.append(jnp.concatenate(out_heads, axis=1))
        o_ref[...] = jnp.concatenate(out_rows, axis=0).astype(o_ref.dtype)

    wq, wv = RET_HEADS * RET_DK, RET_HEADS * RET_DV
    return pl.pallas_call(
        body,
        grid=(DEC_BATCH // nb,),
        in_specs=[
            pl.BlockSpec(memory_space=pltpu.SMEM),
            pl.BlockSpec((nb * P, wq), lambda g: (row_blk0 + g, 0)),
            pl.BlockSpec((nb * P, wq), lambda g: (row_blk0 + g, 0)),
            pl.BlockSpec((nb * P, wv), lambda g: (row_blk0 + g, 0)),
            pl.BlockSpec((nb * P, wv), lambda g: (row_blk0 + g, 0)),
            pl.BlockSpec((1, wv), lambda g: (0, 0)),
            pl.BlockSpec((nb, RET_HEADS, RET_DK, RET_DV), lambda g: (g, 0, 0, 0)),
            pl.BlockSpec(memory_space=pl.ANY),
        ],
        out_specs=[
            pl.BlockSpec((nb * P, wv), lambda g: (row_blk0 + g, 0)),
            pl.BlockSpec((nb, RET_HEADS, RET_DK, RET_DV), lambda g: (g, 0, 0, 0)),
        ],
        out_shape=[
            jax.ShapeDtypeStruct((ROWS, wv), BF16),
            jax.ShapeDtypeStruct((DEC_BATCH, RET_HEADS, RET_DK, RET_DV), F32),
        ],
        input_output_aliases={7: 0},
        compiler_params=_params("arbitrary"),
        name="retention_sample",
    )(log_g, rq, rk, rv, rg, gn_w, state_ret, ret_o)


def _moba_prompt(mq, mk, mv, cache_k, page_table):
    BL = MOBA_BLOCK
    nblk = SEQ // BL
    TQ = 2 * BL
    scale = MOBA_DH ** -0.5

    n_sblk = PAST_LEN // MOBA_BLOCK
    ppb = MOBA_BLOCK // PAGE_SIZE
    UNITS = DEC_BATCH * n_sblk
    n_tiles = SEQ // TQ
    trips_per_head = (n_tiles // 2) ** 2
    U_TILE, U_TRIP = 4, 3
    assert MOBA_HEADS * (n_tiles * U_TILE + trips_per_head * U_TRIP) == UNITS
    AHEAD = 12
    RING = AHEAD + max(U_TILE, U_TRIP)
    assert RING & (RING - 1) == 0 and UNITS & (UNITS - 1) == 0 and n_sblk & (n_sblk - 1) == 0

    def body(pt_ref, q_ref, k_ref, v_ref, _, ck_ref, o_ref, km_ref, kaug, vt, kmean, s_a, s_b, p_a, p_b,
             ring, sem):
        h, ti = pl.program_id(0), pl.program_id(1)

        def unit_copies(u):
            uu = u & (UNITS - 1)
            b, j = uu // n_sblk, uu & (n_sblk - 1)
            slot = u & (RING - 1)
            return [pltpu.make_async_copy(ck_ref.at[pt_ref[b, j * ppb + pg]], ring.at[slot, pg],
                                          sem.at[slot, pg]) for pg in range(ppb)]

        def fetch(u0, n):
            for i in range(n):
                for cp in unit_copies(u0 + AHEAD + i):
                    cp.start()

        def reduce(u0, n):
            for i in range(n):
                u = u0 + i
                for cp in unit_copies(u):
                    cp.wait()
                slot = u & (RING - 1)
                lanes8 = (8, PAGE_SIZE // 8, MOBA_HEADS, MOBA_DH)
                parts = [jnp.sum(ring[slot, pg].reshape(lanes8), axis=1) for pg in range(ppb)]
                tot = jnp.sum(functools.reduce(lambda a, b: a + b, parts), axis=0)
                km_ref[u // n_sblk, u & (n_sblk - 1)] = tot * (1.0 / MOBA_BLOCK)

        @pl.when(jnp.logical_and(h == 0, ti == 0))
        def _():
            fetch(-AHEAD, AHEAD)

        u_tile = (h * n_tiles + ti) * U_TILE + (h * trips_per_head + (ti * ti) // 4) * U_TRIP
        fetch(u_tile, U_TILE)
        reduce(u_tile, U_TILE)

        @pl.when(ti == 0)
        def _():
            kmean[...] = jnp.zeros_like(kmean)
            lane = _iota((BL, LANES), 1)

            def prep(jb, carry):
                r0 = pl.multiple_of(jb * BL, BL)
                kb = k_ref[pl.ds(r0, BL), :]
                kmean[pl.ds(jb, 1), :] = jnp.mean(kb, axis=0, keepdims=True)
                kaug[pl.ds(r0, BL), 0:MOBA_DH] = kb.astype(BF16)
                kaug[pl.ds(r0, BL), MOBA_DH:2 * MOBA_DH] = (lane == jb).astype(BF16)
                vt[:, pl.ds(r0, BL)] = v_ref[pl.ds(r0, BL), :].T.astype(BF16)
                return carry

            lax.fori_loop(0, nblk, prep, 0)

        q = q_ref[...]
        gt = lax.dot_general(kmean[0:nblk, :], q, _NT, precision=lax.Precision.HIGHEST,
                             preferred_element_type=F32)
        n_sg = nblk // 8
        row = _iota((nblk, TQ), 0)
        row8 = _iota((8, TQ), 0)
        second = jnp.where(_iota((8, TQ), 1) >= BL, 1, 0)
        own = 2 * ti + second
        gtv = jnp.where(row < 2 * ti + jnp.where(_iota((nblk, TQ), 1) >= BL, 1, 0), gt, -jnp.inf)
        cnts = [jnp.zeros((8, TQ), jnp.int32) for _ in range(n_sg)]
        for blk in range(nblk):
            g_blk = gtv[blk:blk + 1, :]
            for sg in range(n_sg):
                g_sg = gt[8 * sg:8 * sg + 8, :]
                if 8 * sg > blk:
                    cnts[sg] = cnts[sg] + jnp.where(g_blk >= g_sg, 1, 0)
                elif 8 * sg + 7 < blk:
                    cnts[sg] = cnts[sg] + jnp.where(g_blk > g_sg, 1, 0)
                else:
                    cnts[sg] = cnts[sg] + jnp.where(row8 + 8 * sg > blk, jnp.where(g_blk >= g_sg, 1, 0),
                                                    jnp.where(g_blk > g_sg, 1, 0))
        picked = [(cnts[sg] < MOBA_TOPK) & (row8 + 8 * sg < own) for sg in range(n_sg)]
        qs = (q * (scale * LOG2_E)).astype(BF16)

        def with_bias(keeps):
            parts = [jnp.where(kp, 0.0, NEG) for kp in keeps] + [jnp.zeros((LANES - nblk, TQ), F32)]
            return jnp.concatenate([qs, jnp.concatenate(parts, axis=0).T.astype(BF16)], axis=1)

        qa_past = with_bias([picked[sg] & (row8 + 8 * sg < 2 * ti) for sg in range(n_sg)])
        qa_own = with_bias([picked[sg] | (row8 + 8 * sg == own) for sg in range(n_sg)])

        def scores(qx, g):
            c0 = pl.multiple_of(g * TQ, TQ)
            return lax.dot_general(kaug[pl.ds(c0, TQ), :], qx, _NT, preferred_element_type=F32)

        def values(g):
            c0 = pl.multiple_of(g * TQ, TQ)
            return vt[:, pl.ds(c0, TQ)]

        CH = 64

        def fold8(x, fn):
            return fn(x.reshape(CH // 8, 8, TQ), axis=0)

        def softmax_stage(s_ref, p_ref, m, l):
            mx8 = jnp.broadcast_to(m, (8, TQ))
            for c in range(TQ // CH):
                mx8 = jnp.maximum(mx8, fold8(s_ref[c * CH:(c + 1) * CH, :], jnp.max))
            m_new = jnp.max(mx8, axis=0, keepdims=True)
            alpha = jnp.exp2(m - m_new)
            sum8 = jnp.zeros((8, TQ), F32)
            for c in range(TQ // CH):
                pc = jnp.exp2(s_ref[c * CH:(c + 1) * CH, :] - m_new)
                sum8 = sum8 + fold8(pc, jnp.sum)
                p_ref[c * CH:(c + 1) * CH, :] = pc.astype(BF16)
            return m_new, l * alpha + jnp.sum(sum8, axis=0, keepdims=True), alpha

        last = SEQ // TQ - 1

        s_a[...] = jnp.where(_iota((TQ, TQ), 0) <= _iota((TQ, TQ), 1), scores(qa_own, ti), NEG)
        m, l, _ = softmax_stage(s_a, p_a, jnp.full((1, TQ), NEG, F32), jnp.zeros((1, TQ), F32))
        s_b[...] = scores(qa_past, 0)
        acc = jnp.zeros((MOBA_DH, TQ), F32)

        def trip(k, carry):
            m, l, acc = carry
            g0 = 2 * k
            u_trip = u_tile + U_TILE + k * U_TRIP
            fetch(u_trip, U_TRIP)
            s_a[...] = scores(qa_past, jnp.minimum(g0 + 1, last))
            pv = jnp.dot(values(jnp.where(k == 0, ti, g0 - 1)), p_a[...], preferred_element_type=F32)
            m, l, alpha = softmax_stage(s_b, p_b, m, l)
            acc = (acc + pv) * alpha
            s_b[...] = scores(qa_past, jnp.minimum(g0 + 2, last))
            pv = jnp.dot(values(g0), p_b[...], preferred_element_type=F32)
            m, l, alpha = softmax_stage(s_a, p_a, m, l)
            reduce(u_trip, U_TRIP)
            return m, l, (acc + pv) * alpha

        trips = (ti + 1) // 2
        m, l, acc = lax.fori_loop(0, trips, trip, (m, l, acc))
        acc = acc + jnp.dot(values(jnp.where(trips == 0, ti, 2 * trips - 1)), p_a[...],
                            preferred_element_type=F32)
        o_ref[...] = (acc / l).T.astype(o_ref.dtype)

        @pl.when(jnp.logical_and(h == MOBA_HEADS - 1, ti == n_tiles - 1))
        def _():
            for u in range(UNITS, UNITS + AHEAD):
                for cp in unit_copies(u):
                    cp.wait()

    grid_spec = pltpu.PrefetchScalarGridSpec(
        num_scalar_prefetch=1,
        grid=(MOBA_HEADS, n_tiles),
        in_specs=[
            pl.BlockSpec((TQ, MOBA_DH), lambda h, i, pt: (i, h)),
            pl.BlockSpec((SEQ, MOBA_DH), lambda h, i, pt: (0, h)),
            pl.BlockSpec((SEQ, MOBA_DH), lambda h, i, pt: (0, h)),
            pl.BlockSpec(memory_space=pl.ANY),
            pl.BlockSpec(memory_space=pl.ANY),
        ],
        out_specs=[
            pl.BlockSpec((TQ, MOBA_DH), lambda h, i, pt: (i, h)),
            pl.BlockSpec((DEC_BATCH, n_sblk, MOBA_HEADS, MOBA_DH), lambda h, i, pt: (0, 0, 0, 0)),
        ],
        scratch_shapes=[
            pltpu.VMEM((SEQ, 2 * MOBA_DH), BF16),
            pltpu.VMEM((MOBA_DH, SEQ), BF16),
            pltpu.VMEM((LANES, MOBA_DH), F32),
            pltpu.VMEM((TQ, TQ), F32),
            pltpu.VMEM((TQ, TQ), F32),
            pltpu.VMEM((TQ, TQ), BF16),
            pltpu.VMEM((TQ, TQ), BF16),
            pltpu.VMEM((RING, ppb, PAGE_SIZE, MOBA_HEADS, MOBA_DH), F32),
            pltpu.SemaphoreType.DMA((RING, ppb)),
        ],
    )
    return pl.pallas_call(
        body,
        grid_spec=grid_spec,
        out_shape=[
            jax.ShapeDtypeStruct((ROWS, MOBA_HEADS * MOBA_DH), BF16),
            jax.ShapeDtypeStruct((DEC_BATCH, n_sblk, MOBA_HEADS, MOBA_DH), F32),
        ],
        input_output_aliases={4: 0},
        compiler_params=_params("arbitrary", "arbitrary"),
        name="moba_prompt",
    )(page_table, mq, mk, mv, _zero_slab(MOBA_HEADS * MOBA_DH), cache_k)


def _sample_block_select(mq, kmean):
    nblk = PAST_LEN // MOBA_BLOCK
    P = DEC_PAD
    row_blk0 = SEQ // P

    def body(q_ref, km_ref, o_ref):
        q = q_ref[...]
        lane = _iota((P, LANES), 1)
        for h in range(MOBA_HEADS):
            cols = slice(h * MOBA_DH, (h + 1) * MOBA_DH)
            kmh = jnp.concatenate([km_ref[0, :, h, :], jnp.zeros((LANES - nblk, MOBA_DH), F32)], axis=0)
            gate = lax.dot_general(q[:, cols], kmh, _NT, precision=lax.Precision.HIGHEST,
                                   preferred_element_type=F32)
            g = jnp.where(lane < nblk, gate, -jnp.inf)
            ids = []
            for _ in range(MOBA_TOPK):
                mx = jnp.max(g, axis=-1, keepdims=True)
                first = jnp.min(jnp.where(g == mx, lane, 1 << 20), axis=-1, keepdims=True)
                ids.append(first)
                g = jnp.where(lane == first, -jnp.inf, g)
            o_ref[0, h] = jnp.where(lane == 0, ids[0], jnp.where(lane == 1, ids[1], ids[2]))

    return pl.pallas_call(
        body,
        grid=(DEC_BATCH,),
        in_specs=[
            pl.BlockSpec((P, MOBA_HEADS * MOBA_DH), lambda b: (row_blk0 + b, 0)),
            pl.BlockSpec((1, nblk, MOBA_HEADS, MOBA_DH), lambda b: (b, 0, 0, 0)),
        ],
        out_specs=pl.BlockSpec((1, MOBA_HEADS, P, LANES), lambda b: (b, 0, 0, 0)),
        out_shape=jax.ShapeDtypeStruct((DEC_BATCH, MOBA_HEADS, P, LANES), jnp.int32),
        compiler_params=_params("arbitrary"),
        name="sample_block_select",
    )(mq, kmean)


def _sample_attend(mq, mk, mv, cache_k, cache_v, page_table, blk_ids, moba_o):
    ppb = MOBA_BLOCK // PAGE_SIZE
    n_sel = MOBA_TOPK * MOBA_BLOCK
    nb = SEQS_PER_STEP
    n_q = nb * DEC_SEQ
    n_copies = n_q * MOBA_TOPK * ppb
    n_steps = (DEC_BATCH // nb) * MOBA_HEADS
    scale = MOBA_DH ** -0.5
    P = DEC_PAD
    row_blk0 = SEQ // (nb * P)

    def body(pt_ref, ids_ref, q_ref, kn_ref, vn_ref, ck_ref, cv_ref, _, o_ref, kbuf, vbuf, sem):
        step = pl.program_id(0) * MOBA_HEADS + pl.program_id(1)

        def copies(st, slot):
            g, h = st // MOBA_HEADS, st % MOBA_HEADS
            out = []
            for qi in range(n_q):
                b, t = g * nb + qi // DEC_SEQ, qi % DEC_SEQ
                for sl in range(MOBA_TOPK):
                    blk = ids_ref[((b * DEC_SEQ + t) * MOBA_HEADS + h) * MOBA_TOPK + sl]
                    for pg in range(ppb):
                        phys = pt_ref[b, blk * ppb + pg]
                        n = (qi * MOBA_TOPK + sl) * ppb + pg
                        rows = pl.ds((sl * ppb + pg) * PAGE_SIZE, PAGE_SIZE)
                        for src, dst, which in ((ck_ref, kbuf, 0), (cv_ref, vbuf, 1)):
                            out.append(pltpu.make_async_copy(src.at[phys, :, h, :], dst.at[slot, qi, rows, :],
                                                             sem.at[slot, which, n]))
            return out

        slot = step % 2

        @pl.when(step == 0)
        def _():
            for n, cp in enumerate(copies(step, 0)):
                cp.start(priority=n % 2)

        @pl.when(step + 1 < n_steps)
        def _():
            for n, cp in enumerate(copies(step + 1, 1 - slot)):
                cp.start(priority=n % 2)

        for cp in copies(step, slot):
            cp.wait()

        q, kn, vn = q_ref[...], kn_ref[...], vn_ref[...]
        rown = _iota((P, 1), 0)
        out_rows = []
        for bb in range(nb):
            knb, vnb = kn[bb * P:(bb + 1) * P], vn[bb * P:(bb + 1) * P]
            res = jnp.zeros((P, MOBA_DH), F32)
            for t in range(DEC_SEQ):
                qi = bb * DEC_SEQ + t
                qt = q[bb * P + t:bb * P + t + 1, :] * scale
                s = jnp.sum(kbuf[slot, qi] * qt, axis=-1, keepdims=True)
                sn = jnp.where(rown <= t, jnp.sum(knb * qt, axis=-1, keepdims=True), NEG)
                m = jnp.maximum(jnp.max(s, axis=0, keepdims=True), jnp.max(sn, axis=0, keepdims=True))
                p, pn = jnp.exp(s - m), jnp.exp(sn - m)
                l = jnp.sum(p, axis=0, keepdims=True) + jnp.sum(pn, axis=0, keepdims=True)
                o = (jnp.sum(p * vbuf[slot, qi], axis=0, keepdims=True)
                     + jnp.sum(pn * vnb, axis=0, keepdims=True))
                res = jnp.where(rown == t, o / l, res)
            out_rows.append(res)
        o_ref[...] = jnp.concatenate(out_rows, axis=0).astype(o_ref.dtype)

    head_blk = pl.BlockSpec((nb * P, MOBA_DH), lambda g, h, pt, ids: (row_blk0 + g, h))
    any_spec = pl.BlockSpec(memory_space=pl.ANY)
    grid_spec = pltpu.PrefetchScalarGridSpec(
        num_scalar_prefetch=2,
        grid=(DEC_BATCH // nb, MOBA_HEADS),
        in_specs=[head_blk, head_blk, head_blk, any_spec, any_spec, any_spec],
        out_specs=head_blk,
        scratch_shapes=[
            pltpu.VMEM((2, n_q, n_sel, MOBA_DH), F32),
            pltpu.VMEM((2, n_q, n_sel, MOBA_DH), F32),
            pltpu.SemaphoreType.DMA((2, 2, n_copies)),
        ],
    )
    return pl.pallas_call(
        body,
        grid_spec=grid_spec,
        out_shape=jax.ShapeDtypeStruct((ROWS, MOBA_HEADS * MOBA_DH), BF16),
        input_output_aliases={7: 0},
        compiler_params=_params("arbitrary", "arbitrary"),
        name="sample_attend",
    )(page_table, blk_ids, mq, mk, mv, cache_k, cache_v, moba_o)


def _softmax_attend(q, k, v):
    s = lax.dot_general(q, k, _NT, preferred_element_type=F32)
    m = jnp.max(s, axis=-1, keepdims=True)
    p = jnp.exp(s - m)
    l = jnp.sum(p, axis=-1, keepdims=True)
    return jnp.dot(p.astype(v.dtype), v, preferred_element_type=F32) / l


def _mem_attend_prompt(cq, mem_k, mem_v):
    tq = 512
    scale = MEM_DH ** -0.5

    def body(q_ref, k_ref, v_ref, _, o_ref):
        q = (q_ref[...] * scale).astype(BF16)
        o_ref[...] = _softmax_attend(q, k_ref[...].astype(BF16), v_ref[...].astype(BF16)).astype(o_ref.dtype)

    kv_spec = pl.BlockSpec((N_MEM, MEM_DH), lambda r, h: (0, h))
    return pl.pallas_call(
        body,
        grid=(SEQ // tq, MEM_HEADS),
        in_specs=[pl.BlockSpec((tq, MEM_DH), lambda r, h: (r, h)), kv_spec, kv_spec,
                  pl.BlockSpec(memory_space=pl.ANY)],
        out_specs=pl.BlockSpec((tq, MEM_DH), lambda r, h: (r, h)),
        out_shape=jax.ShapeDtypeStruct((ROWS, MEM_HEADS * MEM_DH), BF16),
        input_output_aliases={3: 0},
        compiler_params=_params("arbitrary", "arbitrary"),
        name="mem_attend_prompt",
    )(cq, mem_k, mem_v, _zero_slab(MEM_HEADS * MEM_DH))


def _mem_attend_sample(cq, mem_k, mem_v, mem_o):
    nb = SEQS_PER_STEP
    P = DEC_PAD
    w = MEM_HEADS * MEM_DH
    scale = MEM_DH ** -0.5
    row_blk0 = SEQ // (nb * P)

    def body(q_ref, k_ref, v_ref, _, o_ref):
        qa = q_ref[...] * scale
        out_rows = []
        for bb in range(nb):
            heads = []
            for h in range(MEM_HEADS):
                cols = slice(h * MEM_DH, (h + 1) * MEM_DH)
                heads.append(_softmax_attend(qa[bb * P:(bb + 1) * P, cols], k_ref[bb, :, h, :],
                                             v_ref[bb, :, h, :]))
            out_rows.append(jnp.concatenate(heads, axis=1))
        o_ref[...] = jnp.concatenate(out_rows, axis=0).astype(o_ref.dtype)

    kv_spec = pl.BlockSpec((nb, N_MEM, MEM_HEADS, MEM_DH), lambda g: (g, 0, 0, 0))
    row_spec = pl.BlockSpec((nb * P, w), lambda g: (row_blk0 + g, 0))
    return pl.pallas_call(
        body,
        grid=(DEC_BATCH // nb,),
        in_specs=[row_spec, kv_spec, kv_spec, pl.BlockSpec(memory_space=pl.ANY)],
        out_specs=row_spec,
        out_shape=jax.ShapeDtypeStruct((ROWS, w), BF16),
        input_output_aliases={3: 0},
        compiler_params=_params("arbitrary"),
        name="mem_attend_sample",
    )(cq, mem_k, mem_v, mem_o)


def kernel(x_prompt, x_sample, cache_k, cache_v, cache_mem_k, cache_mem_v, state_ret, page_table, mem_prompt, attn_norm_w, w_in, moba_q_norm, moba_k_norm, mem_q_norm, mem_k_norm, mem_norm_w, w_mem_kv, ret_gn_w, w_ret_o, w_moba_o, w_mem_o, w_out, mlp_norm_w, w_up, w_down):
    hd_mem = MEM_HEADS * MEM_DH

    xs = jnp.pad(x_sample, ((0, 0), (0, DEC_PAD - DEC_SEQ), (0, 0))).reshape(ROWS_S, D_MODEL)
    x = jnp.concatenate([x_prompt[0], xs], axis=0)

    half = RET_DK // 2
    inv = ROPE_BASE ** (-jnp.arange(half, dtype=F32) * (2.0 / RET_DK))
    inv = jnp.concatenate([inv, inv]).reshape(1, LANES)
    log_g = jnp.log1p(-jnp.exp2(-5.0 - jnp.arange(RET_HEADS, dtype=F32)))
    cos_t, sin_t = _rope_tables(inv)

    xn = _rmsnorm_cast(x, attn_norm_w, 384)

    rope_args = [(cos_t, (TM, LANES), lambda i, j: (i, 0)), (sin_t, (TM, LANES), lambda i, j: (i, 0))]

    def norm_arg(wv):
        return [(wv.reshape(1, -1), (1, wv.shape[0]), lambda i, j: (0, 0))]

    def proj(col0, n_cols, epi, args, dt, name):
        return _matmul(xn, w_in, col0, n_cols, TM, TN, epi, args, [dt], name)

    rq = proj(C_RQ, 1024, functools.partial(_epi_rope, 1.0), rope_args, BF16, "proj_rq")
    rk = proj(C_RK, 1024, functools.partial(_epi_rope, RET_DK ** -0.5), rope_args, BF16, "proj_rk")
    rv = proj(C_RV, 2048, _epi_plain, [], BF16, "proj_rv")
    rg = proj(C_RG, 2048, _epi_plain, [], F32, "proj_rg")
    mq = proj(C_MQ, 1024, functools.partial(_epi_headnorm, MOBA_DH), norm_arg(moba_q_norm), F32, "proj_mq")
    mk = proj(C_MK, 1024, functools.partial(_epi_headnorm, MOBA_DH), norm_arg(moba_k_norm), F32, "proj_mk")
    mv = proj(C_MV, 1024, _epi_plain, [], F32, "proj_mv")
    cq = proj(C_CQ, 1024, functools.partial(_epi_headnorm, MEM_DH), norm_arg(mem_q_norm), F32, "proj_cq")
    gates = proj(C_GATE, 3 * D_MODEL, _epi_sigmoid, [], BF16, "proj_gates")

    mn = _rmsnorm_cast(mem_prompt[0], mem_norm_w, N_MEM)
    mem_k_p = _matmul(mn, w_mem_kv, 0, hd_mem, N_MEM, TN, functools.partial(_epi_headnorm, MEM_DH),
                      norm_arg(mem_k_norm), [F32], "mem_k")
    mem_v_p = _matmul(mn, w_mem_kv, hd_mem, hd_mem, N_MEM, TN, _epi_plain, [], [F32], "mem_v")

    gn_w = ret_gn_w.reshape(1, -1)
    ret_o, ret_state_p = _retention_prompt(log_g, rq, rk, rv, rg, gn_w)
    ret_o, ret_state_s = _retention_sample(log_g, rq, rk, rv, rg, gn_w, state_ret, ret_o)

    moba_o, kmean_s = _moba_prompt(mq, mk, mv, cache_k, page_table)
    ids = _sample_block_select(mq, kmean_s)
    ids = ids[:, :, :DEC_SEQ, :MOBA_TOPK].transpose(0, 2, 1, 3).reshape(-1)
    moba_o = _sample_attend(mq, mk, mv, cache_k, cache_v, page_table, ids, moba_o)

    mem_o = _mem_attend_prompt(cq, mem_k_p, mem_v_p)
    mem_o = _mem_attend_sample(cq, cache_mem_k, cache_mem_v, mem_o)

    tn = TN_EPI
    merged = _gated_branch_merge([ret_o, moba_o, mem_o],
                                 [w_ret_o.astype(BF16), w_moba_o.astype(BF16), w_mem_o.astype(BF16)], gates)
    hres = _matmul(merged, w_out, 0, D_MODEL, TM, tn, _epi_residual,
                   [(x, (TM, tn), lambda i, j: (i, j))], [F32], "out_proj")
    hn = _rmsnorm_cast(hres, mlp_norm_w, 384)
    u2 = _matmul(hn, w_up, 0, D_FF, TM, TN, _epi_relu_sq, [], [BF16], "mlp_up")
    tm_down = TM // 2
    y = _matmul(u2, w_down.astype(BF16), 0, D_MODEL, tm_down, tn, _epi_residual,
                [(hres, (tm_down, tn), lambda i, j: (i, j))], [F32], "mlp_down")

    def sample_rows(a):
        return a.reshape(DEC_BATCH, DEC_PAD, *a.shape[1:])[:, :DEC_SEQ]

    shape4 = (MOBA_HEADS, MOBA_DH)
    return (
        y[:SEQ][None],
        sample_rows(y[SEQ:]),
        mk[:SEQ].reshape(1, SEQ, *shape4),
        mv[:SEQ].reshape(1, SEQ, *shape4),
        ret_state_p[None],
        mem_k_p.reshape(1, N_MEM, MEM_HEADS, MEM_DH),
        mem_v_p.reshape(1, N_MEM, MEM_HEADS, MEM_DH),
        sample_rows(mk[SEQ:]).reshape(DEC_BATCH, DEC_SEQ, *shape4),
        sample_rows(mv[SEQ:]).reshape(DEC_BATCH, DEC_SEQ, *shape4),
        ret_state_s,
    )
```

```python
import functools

import jax
import jax.numpy as jnp
from jax import lax
from jax.experimental import pallas as pl
from jax.experimental.pallas import tpu as pltpu

F32 = jnp.float32
BF16 = jnp.bfloat16

D_MODEL = 2048
SEQ = 8192
DEC_BATCH = 32
DEC_SEQ = 4
DEC_PAD = 8
PAST_LEN = 16384
PAGE_SIZE = 128
N_MEM = 256
RET_HEADS = 8
RET_DK = 128
RET_DV = 256
RET_CHUNK = 128
ROPE_BASE = 10000.0
MOBA_HEADS = 8
MOBA_DH = 128
MOBA_BLOCK = 256
MOBA_TOPK = 3
MEM_HEADS = 4
MEM_DH = 256
D_FF = 4 * D_MODEL
EPS = 1e-6
GN_EPS = 1e-5
NEG = -1e30
LOG2_E = 1.4426950408889634

ROWS_S = DEC_BATCH * DEC_PAD
ROWS = SEQ + ROWS_S
TM = 1056
TN = 1024
TN_EPI = 512
LANES = 128
SEQS_PER_STEP = 2
VMEM_LIMIT = 56 * 1024 * 1024

C_RQ, C_RK, C_RV, C_RG = 0, 1024, 2048, 4096
C_MQ, C_MK, C_MV, C_CQ, C_GATE = 6144, 7168, 8192, 9216, 10240

_NT = (((1,), (1,)), ((), ()))


def _iota(shape, dim):
    return lax.broadcasted_iota(jnp.int32, shape, dim)


def _params(*sem):
    return pltpu.CompilerParams(dimension_semantics=sem, vmem_limit_bytes=VMEM_LIMIT)


def _zero_slab(width):
    return jnp.zeros((ROWS, width), BF16)


def _rmsnorm_cast(x, w, tm):
    rows, d = x.shape

    def body(x_ref, w_ref, o_ref):
        xf = x_ref[...]
        y = xf * lax.rsqrt(jnp.mean(xf * xf, axis=-1, keepdims=True) + EPS)
        o_ref[...] = (y * w_ref[...]).astype(o_ref.dtype)

    return pl.pallas_call(
        body,
        grid=(rows // tm,),
        in_specs=[pl.BlockSpec((tm, d), lambda i: (i, 0)), pl.BlockSpec((1, d), lambda i: (0, 0))],
        out_specs=pl.BlockSpec((tm, d), lambda i: (i, 0)),
        out_shape=jax.ShapeDtypeStruct((rows, d), BF16),
        compiler_params=_params("arbitrary"),
        name="rmsnorm_cast",
    )(x, w.reshape(1, d))


def _stack_rmsnorm(xp, xs, w):
    tm = ROWS_S
    n_p = SEQ // tm
    d = xp.shape[1]

    def body(xp_ref, xs_ref, w_ref, x_ref, o_ref):
        xf = jnp.where(pl.program_id(0) < n_p, xp_ref[...], xs_ref[...])
        x_ref[...] = xf
        y = xf * lax.rsqrt(jnp.mean(xf * xf, axis=-1, keepdims=True) + EPS)
        o_ref[...] = (y * w_ref[...]).astype(o_ref.dtype)

    return pl.pallas_call(
        body,
        grid=(n_p + 1,),
        in_specs=[pl.BlockSpec((tm, d), lambda i: (jnp.minimum(i, n_p - 1), 0)),
                  pl.BlockSpec((tm, d), lambda i: (0, 0)),
                  pl.BlockSpec((1, d), lambda i: (0, 0))],
        out_specs=[pl.BlockSpec((tm, d), lambda i: (i, 0))] * 2,
        out_shape=[jax.ShapeDtypeStruct((ROWS, d), F32), jax.ShapeDtypeStruct((ROWS, d), BF16)],
        compiler_params=_params("arbitrary"),
        name="stack_rmsnorm",
    )(xp, xs, w.reshape(1, d))


def _rope_tables(inv):
    tm = TM

    def body(inv_ref, c_ref, s_ref):
        row = pl.program_id(0) * tm + _iota((tm, LANES), 0)
        pos = jnp.where(row < SEQ, row, PAST_LEN + ((row - SEQ) & (DEC_PAD - 1)))
        ang = pos.astype(F32) * inv_ref[...]
        sn = jnp.sin(ang)
        c_ref[...] = jnp.cos(ang)
        s_ref[...] = jnp.where(_iota((tm, LANES), 1) < RET_DK // 2, -sn, sn)

    return pl.pallas_call(
        body,
        grid=(ROWS // tm,),
        in_specs=[pl.BlockSpec((1, LANES), lambda i: (0, 0))],
        out_specs=[pl.BlockSpec((tm, LANES), lambda i: (i, 0))] * 2,
        out_shape=[jax.ShapeDtypeStruct((ROWS, LANES), F32)] * 2,
        compiler_params=_params("arbitrary"),
        name="rope_tables",
    )(inv)


def _matmul(a, w, col0, n_cols, tm, tn, epi_fn, epi_args, out_dtypes, name):
    m, k = a.shape
    assert m % tm == 0 and n_cols % tn == 0 and col0 % tn == 0
    jb0 = col0 // tn
    n_epi, n_out = len(epi_args), len(out_dtypes)
    cast_once = w.dtype != BF16

    def body(a_ref, w_ref, *rest):
        epi_refs, out_refs = rest[:n_epi], rest[n_epi:n_epi + n_out]
        if cast_once:
            wb_ref = rest[-1]

            @pl.when(pl.program_id(1) == 0)
            def _():
                wb_ref[...] = w_ref[...].astype(BF16)
        else:
            wb_ref = w_ref
        acc = jnp.dot(a_ref[...], wb_ref[...], preferred_element_type=F32)
        for o_ref, o in zip(out_refs, epi_fn(acc, *epi_refs)):
            o_ref[...] = o.astype(o_ref.dtype)

    def spec(block, index_map):
        if cast_once:
            return pl.BlockSpec(block, lambda j, i: index_map(i, j))
        return pl.BlockSpec(block, index_map)

    in_specs = [spec((tm, k), lambda i, j: (i, 0)), spec((k, tn), lambda i, j: (0, jb0 + j))]
    in_specs += [spec(bs, im) for _, bs, im in epi_args]
    n_i, n_j = m // tm, n_cols // tn
    outs = pl.pallas_call(
        body,
        grid=(n_j, n_i) if cast_once else (n_i, n_j),
        in_specs=in_specs,
        out_specs=[spec((tm, tn), lambda i, j: (i, j)) for _ in out_dtypes],
        out_shape=[jax.ShapeDtypeStruct((m, n_cols), dt) for dt in out_dtypes],
        scratch_shapes=[pltpu.VMEM((k, tn), BF16)] if cast_once else [],
        compiler_params=_params("arbitrary", "arbitrary"),
        name=name,
    )(a, w, *[x for x, _, _ in epi_args])
    return outs[0] if n_out == 1 else outs


def _epi_plain(acc):
    return (acc,)


def _epi_sigmoid(acc):
    return (1.0 / (1.0 + jnp.exp(-acc)),)


def _epi_relu_sq(acc):
    u = jnp.maximum(acc, 0.0)
    return (u * u,)


def _epi_rope(scale, acc, c_ref, s_ref):
    c, s = c_ref[...], s_ref[...]
    pieces = []
    for hh in range(acc.shape[1] // RET_DK):
        x = acc[:, hh * RET_DK:(hh + 1) * RET_DK]
        y = x * c + pltpu.roll(x, RET_DK // 2, 1) * s
        pieces.append(y * scale if scale != 1.0 else y)
    return (jnp.concatenate(pieces, axis=1),)


def _epi_headnorm(hw, acc, w_ref):
    w = w_ref[...]
    pieces = []
    for hh in range(acc.shape[1] // hw):
        x = acc[:, hh * hw:(hh + 1) * hw]
        pieces.append(x * lax.rsqrt(jnp.mean(x * x, axis=-1, keepdims=True) + EPS) * w)
    return (jnp.concatenate(pieces, axis=1),)


def _epi_residual(acc, x_ref):
    return (x_ref[...] + acc,)


def _gated_branch_merge(branches, weights, gates):
    n_b = len(branches)
    tm, tn = TM, TN_EPI

    def body(*refs):
        a_refs, w_refs, g_refs, o_ref = refs[:n_b], refs[n_b:2 * n_b], refs[2 * n_b:3 * n_b], refs[3 * n_b]
        acc = None
        for a_ref, w_ref, g_ref in zip(a_refs, w_refs, g_refs):
            term = g_ref[...].astype(F32) * jnp.dot(a_ref[...], w_ref[...], preferred_element_type=F32)
            acc = term if acc is None else acc + term
        o_ref[...] = acc.astype(o_ref.dtype)

    in_specs = [pl.BlockSpec((tm, a.shape[1]), lambda i, j: (i, 0)) for a in branches]
    in_specs += [pl.BlockSpec((w.shape[0], tn), lambda i, j: (0, j)) for w in weights]
    in_specs += [pl.BlockSpec((tm, tn), lambda i, j, b=b: (i, b * (D_MODEL // tn) + j)) for b in range(n_b)]
    return pl.pallas_call(
        body,
        grid=(ROWS // tm, D_MODEL // tn),
        in_specs=in_specs,
        out_specs=pl.BlockSpec((tm, tn), lambda i, j: (i, j)),
        out_shape=jax.ShapeDtypeStruct((ROWS, D_MODEL), BF16),
        compiler_params=_params("arbitrary", "arbitrary"),
        name="branch_merge",
    )(*branches, *weights, *([gates] * n_b))


def _gated_groupnorm(o, rg, gw):
    mu = jnp.mean(o, axis=-1, keepdims=True)
    d = o - mu
    var = jnp.mean(d * d, axis=-1, keepdims=True)
    rn = d * lax.rsqrt(var + GN_EPS) * gw
    return rg / (1.0 + jnp.exp(-rg)) * rn


def _retention_prompt(log_g, rq, rk, rv, rg, gn_w):
    C = RET_CHUNK
    nc = SEQ // C
    HP = 4

    def body(lg_ref, q_ref, k_ref, v_ref, rg_ref, gw_ref, _, o_ref, st_ref, state):
        hp, c = pl.program_id(0), pl.program_id(1)

        @pl.when(c == 0)
        def _():
            state[...] = jnp.zeros_like(state)

        diff = (_iota((C, C), 0) - _iota((C, C), 1)).astype(F32)
        ii = _iota((C, 1), 0).astype(F32)
        for hh in range(HP):
            lg = lg_ref[hp * HP + hh]
            q = q_ref[:, hh * RET_DK:(hh + 1) * RET_DK]
            k = k_ref[:, hh * RET_DK:(hh + 1) * RET_DK]
            v = v_ref[:, hh * RET_DV:(hh + 1) * RET_DV]
            dmask = jnp.where(diff >= 0, jnp.exp(jnp.maximum(diff, 0.0) * lg), 0.0)
            s = lax.dot_general(q, k, _NT, preferred_element_type=F32) * dmask
            q_decay = jnp.exp((ii + 1.0) * lg)
            k_decay = jnp.exp((C - 1.0 - ii) * lg)
            c_decay = jnp.exp(jnp.zeros((1, RET_DV), F32) + C * lg)
            s_old = state[hh]
            o = (jnp.dot(s.astype(BF16), v, preferred_element_type=F32)
                 + jnp.dot(q, s_old.astype(BF16), preferred_element_type=F32) * q_decay)
            kd_t = (k.astype(F32) * k_decay).T.astype(BF16)
            s_new = s_old * c_decay + jnp.dot(kd_t, v, preferred_element_type=F32)
            state[hh] = s_new
            cols = slice(hh * RET_DV, (hh + 1) * RET_DV)
            o_ref[:, cols] = _gated_groupnorm(o, rg_ref[:, cols], gw_ref[:, cols]).astype(o_ref.dtype)

        @pl.when(c == nc - 1)
        def _():
            st_ref[...] = state[...]

    return pl.pallas_call(
        body,
        grid=(RET_HEADS // HP, nc),
        in_specs=[
            pl.BlockSpec(memory_space=pltpu.SMEM),
            pl.BlockSpec((C, HP * RET_DK), lambda h, c: (c, h)),
            pl.BlockSpec((C, HP * RET_DK), lambda h, c: (c, h)),
            pl.BlockSpec((C, HP * RET_DV), lambda h, c: (c, h)),
            pl.BlockSpec((C, HP * RET_DV), lambda h, c: (c, h)),
            pl.BlockSpec((1, HP * RET_DV), lambda h, c: (0, h)),
            pl.BlockSpec(memory_space=pl.ANY),
        ],
        out_specs=[
            pl.BlockSpec((C, HP * RET_DV), lambda h, c: (c, h)),
            pl.BlockSpec((HP, RET_DK, RET_DV), lambda h, c: (h, 0, 0)),
        ],
        out_shape=[
            jax.ShapeDtypeStruct((ROWS, RET_HEADS * RET_DV), BF16),
            jax.ShapeDtypeStruct((RET_HEADS, RET_DK, RET_DV), F32),
        ],
        input_output_aliases={6: 0},
        scratch_shapes=[pltpu.VMEM((HP, RET_DK, RET_DV), F32)],
        compiler_params=_params("arbitrary", "arbitrary"),
        name="retention_prompt",
    )(log_g, rq, rk, rv, rg, gn_w, _zero_slab(RET_HEADS * RET_DV))


def _retention_sample(log_g, rq, rk, rv, rg, gn_w, state_ret, ret_o):
    T = DEC_SEQ
    P = DEC_PAD
    nb = SEQS_PER_STEP
    row_blk0 = SEQ // (nb * P)

    def body(lg_ref, q_ref, k_ref, v_ref, rg_ref, gw_ref, st_in_ref, _, o_ref, st_ref):
        qa, ka, va = q_ref[...].astype(F32), k_ref[...].astype(F32), v_ref[...].astype(F32)
        rga, gw = rg_ref[...], gw_ref[...]
        ri, ci = _iota((P, P), 0), _iota((P, P), 1)
        diff = (ri - ci).astype(F32)
        ii = _iota((P, 1), 0)
        iif = ii.astype(F32)
        live = ii < T
        out_rows = []
        for bb in range(nb):
            rows = slice(bb * P, (bb + 1) * P)
            out_heads = []
            for h in range(RET_HEADS):
                lg = lg_ref[h]
                q = qa[rows, h * RET_DK:(h + 1) * RET_DK]
                k = jnp.where(live, ka[rows, h * RET_DK:(h + 1) * RET_DK], 0.0)
                v = jnp.where(live, va[rows, h * RET_DV:(h + 1) * RET_DV], 0.0)
                dmask = jnp.where(diff >= 0, jnp.exp(jnp.maximum(diff, 0.0) * lg), 0.0)
                s = lax.dot_general(q, k, _NT, preferred_element_type=F32) * dmask
                q_decay = jnp.exp((iif + 1.0) * lg)
                k_decay = jnp.exp((T - 1.0 - iif) * lg)
                c_decay = jnp.exp(jnp.zeros((1, RET_DV), F32) + T * lg)
                s_old = st_in_ref[bb, h]
                o = (jnp.dot(s, v, preferred_element_type=F32)
                     + jnp.dot(q, s_old, preferred_element_type=F32) * q_decay)
                upd = lax.dot_general(k * k_decay, v, (((0,), (0,)), ((), ())),
                                      preferred_element_type=F32)
                st_ref[bb, h] = s_old * c_decay + upd
                out_heads.append(_gated_groupnorm(
                    o, rga[rows, h * RET_DV:(h + 1) * RET_DV], gw[:, h * RET_DV:(h + 1) * RET_DV]))
            out_rows.append(jnp.concatenate(out_heads, axis=1))
        o_ref[...] = jnp.concatenate(out_rows, axis=0).astype(o_ref.dtype)

    wq, wv = RET_HEADS * RET_DK, RET_HEADS * RET_DV
    return pl.pallas_call(
        body,
        grid=(DEC_BATCH // nb,),
        in_specs=[
            pl.BlockSpec(memory_space=pltpu.SMEM),
            pl.BlockSpec((nb * P, wq), lambda g: (row_blk0 + g, 0)),
            pl.BlockSpec((nb * P, wq), lambda g: (row_blk0 + g, 0)),
            pl.BlockSpec((nb * P, wv), lambda g: (row_blk0 + g, 0)),
            pl.BlockSpec((nb * P, wv), lambda g: (row_blk0 + g, 0)),
            pl.BlockSpec((1, wv), lambda g: (0, 0)),
            pl.BlockSpec((nb, RET_HEADS, RET_DK, RET_DV), lambda g: (g, 0, 0, 0)),
            pl.BlockSpec(memory_space=pl.ANY),
        ],
        out_specs=[
            pl.BlockSpec((nb * P, wv), lambda g: (row_blk0 + g, 0)),
            pl.BlockSpec((nb, RET_HEADS, RET_DK, RET_DV), lambda g: (g, 0, 0, 0)),
        ],
        out_shape=[
            jax.ShapeDtypeStruct((ROWS, wv), BF16),
            jax.ShapeDtypeStruct((DEC_BATCH, RET_HEADS, RET_DK, RET_DV), F32),
        ],
        input_output_aliases={7: 0},
        compiler_params=_params("arbitrary"),
        name="retention_sample",
    )(log_g, rq, rk, rv, rg, gn_w, state_ret, ret_o)


def _moba_prompt(mq, mk, mv, cache_k, page_table):
    BL = MOBA_BLOCK
    nblk = SEQ // BL
    TQ = 2 * BL
    scale = MOBA_DH ** -0.5

    n_sblk = PAST_LEN // MOBA_BLOCK
    ppb = MOBA_BLOCK // PAGE_SIZE
    UNITS = DEC_BATCH * n_sblk
    n_tiles = SEQ // TQ
    trips_per_head = (n_tiles // 2) ** 2
    U_TILE, U_TRIP = 4, 3
    assert MOBA_HEADS * (n_tiles * U_TILE + trips_per_head * U_TRIP) == UNITS
    AHEAD = 12
    RING = AHEAD + max(U_TILE, U_TRIP)
    assert RING & (RING - 1) == 0 and UNITS & (UNITS - 1) == 0 and n_sblk & (n_sblk - 1) == 0
    sblk_shift = n_sblk.bit_length() - 1

    def body(pt_ref, q_ref, k_ref, v_ref, _, ck_ref, o_ref, km_ref, kaug, vt, kmean, s_a, s_b, p_a, p_b,
             ring, sem):
        h, ti = pl.program_id(0), pl.program_id(1)

        def unit_copies(u):
            uu = u & (UNITS - 1)
            b, j = uu >> sblk_shift, uu & (n_sblk - 1)
            slot = u & (RING - 1)
            return [pltpu.make_async_copy(ck_ref.at[pt_ref[b, j * ppb + pg]], ring.at[slot, pg],
                                          sem.at[slot, pg]) for pg in range(ppb)]

        def fetch(u0, n):
            for i in range(n):
                for cp in unit_copies(u0 + AHEAD + i):
                    cp.start()

        def reduce(u0, n):
            for i in range(n):
                u = u0 + i
                for cp in unit_copies(u):
                    cp.wait()
                slot = u & (RING - 1)
                lanes8 = (8, PAGE_SIZE // 8, MOBA_HEADS, MOBA_DH)
                parts = [jnp.sum(ring[slot, pg].reshape(lanes8), axis=1) for pg in range(ppb)]
                tot = jnp.sum(functools.reduce(lambda a, b: a + b, parts), axis=0)
                km_ref[u >> sblk_shift, u & (n_sblk - 1)] = tot * (1.0 / MOBA_BLOCK)

        @pl.when(jnp.logical_and(h == 0, ti == 0))
        def _():
            fetch(-AHEAD, AHEAD)

        u_tile = (h * n_tiles + ti) * U_TILE + (h * trips_per_head + ((ti * ti) >> 2)) * U_TRIP
        fetch(u_tile, U_TILE)
        reduce(u_tile, U_TILE)

        @pl.when(ti == 0)
        def _():
            kmean[...] = jnp.zeros_like(kmean)
            lane = _iota((BL, LANES), 1)

            def prep(jb, carry):
                r0 = pl.multiple_of(jb * BL, BL)
                kb = k_ref[pl.ds(r0, BL), :]
                kmean[pl.ds(jb, 1), :] = jnp.mean(kb, axis=0, keepdims=True)
                kaug[pl.ds(r0, BL), 0:MOBA_DH] = kb.astype(BF16)
                kaug[pl.ds(r0, BL), MOBA_DH:2 * MOBA_DH] = (lane == jb).astype(BF16)
                vt[:, pl.ds(r0, BL)] = v_ref[pl.ds(r0, BL), :].T.astype(BF16)
                return carry

            lax.fori_loop(0, nblk, prep, 0)

        q = q_ref[...]
        gt = lax.dot_general(kmean[0:nblk, :], q, _NT, precision=lax.Precision.HIGHEST,
                             preferred_element_type=F32)
        n_sg = nblk // 8
        row = _iota((nblk, TQ), 0)
        row8 = _iota((8, TQ), 0)
        second = jnp.where(_iota((8, TQ), 1) >= BL, 1, 0)
        own = 2 * ti + second
        gtv = jnp.where(row < 2 * ti + jnp.where(_iota((nblk, TQ), 1) >= BL, 1, 0), gt, -jnp.inf)
        cnts = [jnp.zeros((8, TQ), jnp.int32) for _ in range(n_sg)]
        for blk in range(nblk):
            g_blk = gtv[blk:blk + 1, :]
            for sg in range(n_sg):
                g_sg = gt[8 * sg:8 * sg + 8, :]
                if 8 * sg > blk:
                    cnts[sg] = cnts[sg] + jnp.where(g_blk >= g_sg, 1, 0)
                elif 8 * sg + 7 < blk:
                    cnts[sg] = cnts[sg] + jnp.where(g_blk > g_sg, 1, 0)
                else:
                    cnts[sg] = cnts[sg] + jnp.where(row8 + 8 * sg > blk, jnp.where(g_blk >= g_sg, 1, 0),
                                                    jnp.where(g_blk > g_sg, 1, 0))
        picked = [(cnts[sg] < MOBA_TOPK) & (row8 + 8 * sg < own) for sg in range(n_sg)]
        qs = (q * (scale * LOG2_E)).astype(BF16)

        def with_bias(keeps):
            parts = [jnp.where(kp, 0.0, NEG) for kp in keeps] + [jnp.zeros((LANES - nblk, TQ), F32)]
            return jnp.concatenate([qs, jnp.concatenate(parts, axis=0).T.astype(BF16)], axis=1)

        qa_past = with_bias([picked[sg] & (row8 + 8 * sg < 2 * ti) for sg in range(n_sg)])
        qa_own = with_bias([picked[sg] | (row8 + 8 * sg == own) for sg in range(n_sg)])

        def scores(qx, g):
            c0 = pl.multiple_of(g * TQ, TQ)
            return lax.dot_general(kaug[pl.ds(c0, TQ), :], qx, _NT, preferred_element_type=F32)

        def values(g):
            c0 = pl.multiple_of(g * TQ, TQ)
            return vt[:, pl.ds(c0, TQ)]

        CH = 64

        def fold8(x, fn):
            return fn(x.reshape(CH // 8, 8, TQ), axis=0)

        def softmax_stage(s_ref, p_ref, m, l):
            mx8 = jnp.broadcast_to(m, (8, TQ))
            for c in range(TQ // CH):
                mx8 = jnp.maximum(mx8, fold8(s_ref[c * CH:(c + 1) * CH, :], jnp.max))
            m_new = jnp.max(mx8, axis=0, keepdims=True)
            alpha = jnp.exp2(m - m_new)
            sum8 = jnp.zeros((8, TQ), F32)
            for c in range(TQ // CH):
                pc = jnp.exp2(s_ref[c * CH:(c + 1) * CH, :] - m_new)
                sum8 = sum8 + fold8(pc, jnp.sum)
                p_ref[c * CH:(c + 1) * CH, :] = pc.astype(BF16)
            return m_new, l * alpha + jnp.sum(sum8, axis=0, keepdims=True), alpha

        last = SEQ // TQ - 1

        s_a[...] = jnp.where(_iota((TQ, TQ), 0) <= _iota((TQ, TQ), 1), scores(qa_own, ti), NEG)
        m, l, _ = softmax_stage(s_a, p_a, jnp.full((1, TQ), NEG, F32), jnp.zeros((1, TQ), F32))
        s_b[...] = scores(qa_past, 0)
        acc = jnp.zeros((MOBA_DH, TQ), F32)

        def trip(k, carry):
            m, l, acc = carry
            g0 = 2 * k
            u_trip = u_tile + U_TILE + k * U_TRIP
            fetch(u_trip, U_TRIP)
            s_a[...] = scores(qa_past, jnp.minimum(g0 + 1, last))
            pv = jnp.dot(values(jnp.where(k == 0, ti, g0 - 1)), p_a[...], preferred_element_type=F32)
            m, l, alpha = softmax_stage(s_b, p_b, m, l)
            acc = (acc + pv) * alpha
            s_b[...] = scores(qa_past, jnp.minimum(g0 + 2, last))
            pv = jnp.dot(values(g0), p_b[...], preferred_element_type=F32)
            m, l, alpha = softmax_stage(s_a, p_a, m, l)
            reduce(u_trip, U_TRIP)
            return m, l, (acc + pv) * alpha

        trips = (ti + 1) >> 1
        m, l, acc = lax.fori_loop(0, trips, trip, (m, l, acc))
        acc = acc + jnp.dot(values(jnp.where(trips == 0, ti, 2 * trips - 1)), p_a[...],
                            preferred_element_type=F32)
        o_ref[...] = (acc / l).T.astype(o_ref.dtype)

        @pl.when(jnp.logical_and(h == MOBA_HEADS - 1, ti == n_tiles - 1))
        def _():
            for u in range(UNITS, UNITS + AHEAD):
                for cp in unit_copies(u):
                    cp.wait()

    grid_spec = pltpu.PrefetchScalarGridSpec(
        num_scalar_prefetch=1,
        grid=(MOBA_HEADS, n_tiles),
        in_specs=[
            pl.BlockSpec((TQ, MOBA_DH), lambda h, i, pt: (i, h)),
            pl.BlockSpec((SEQ, MOBA_DH), lambda h, i, pt: (0, h)),
            pl.BlockSpec((SEQ, MOBA_DH), lambda h, i, pt: (0, h)),
            pl.BlockSpec(memory_space=pl.ANY),
            pl.BlockSpec(memory_space=pl.ANY),
        ],
        out_specs=[
            pl.BlockSpec((TQ, MOBA_DH), lambda h, i, pt: (i, h)),
            pl.BlockSpec((DEC_BATCH, n_sblk, MOBA_HEADS, MOBA_DH), lambda h, i, pt: (0, 0, 0, 0)),
        ],
        scratch_shapes=[
            pltpu.VMEM((SEQ, 2 * MOBA_DH), BF16),
            pltpu.VMEM((MOBA_DH, SEQ), BF16),
            pltpu.VMEM((LANES, MOBA_DH), F32),
            pltpu.VMEM((TQ, TQ), F32),
            pltpu.VMEM((TQ, TQ), F32),
            pltpu.VMEM((TQ, TQ), BF16),
            pltpu.VMEM((TQ, TQ), BF16),
            pltpu.VMEM((RING, ppb, PAGE_SIZE, MOBA_HEADS, MOBA_DH), F32),
            pltpu.SemaphoreType.DMA((RING, ppb)),
        ],
    )
    return pl.pallas_call(
        body,
        grid_spec=grid_spec,
        out_shape=[
            jax.ShapeDtypeStruct((ROWS, MOBA_HEADS * MOBA_DH), BF16),
            jax.ShapeDtypeStruct((DEC_BATCH, n_sblk, MOBA_HEADS, MOBA_DH), F32),
        ],
        input_output_aliases={4: 0},
        compiler_params=_params("arbitrary", "arbitrary"),
        name="moba_prompt",
    )(page_table, mq, mk, mv, _zero_slab(MOBA_HEADS * MOBA_DH), cache_k)


def _sample_block_select(mq, kmean):
    nblk = PAST_LEN // MOBA_BLOCK
    P = DEC_PAD
    row_blk0 = SEQ // P

    def body(q_ref, km_ref, o_ref):
        q = q_ref[...]
        lane = _iota((P, LANES), 1)
        for h in range(MOBA_HEADS):
            cols = slice(h * MOBA_DH, (h + 1) * MOBA_DH)
            kmh = jnp.concatenate([km_ref[0, :, h, :], jnp.zeros((LANES - nblk, MOBA_DH), F32)], axis=0)
            gate = lax.dot_general(q[:, cols], kmh, _NT, precision=lax.Precision.HIGHEST,
                                   preferred_element_type=F32)
            g = jnp.where(lane < nblk, gate, -jnp.inf)
            ids = []
            for _ in range(MOBA_TOPK):
                mx = jnp.max(g, axis=-1, keepdims=True)
                first = jnp.min(jnp.where(g == mx, lane, 1 << 20), axis=-1, keepdims=True)
                ids.append(first)
                g = jnp.where(lane == first, -jnp.inf, g)
            o_ref[0, h] = jnp.where(lane == 0, ids[0], jnp.where(lane == 1, ids[1], ids[2]))

    return pl.pallas_call(
        body,
        grid=(DEC_BATCH,),
        in_specs=[
            pl.BlockSpec((P, MOBA_HEADS * MOBA_DH), lambda b: (row_blk0 + b, 0)),
            pl.BlockSpec((1, nblk, MOBA_HEADS, MOBA_DH), lambda b: (b, 0, 0, 0)),
        ],
        out_specs=pl.BlockSpec((1, MOBA_HEADS, P, LANES), lambda b: (b, 0, 0, 0)),
        out_shape=jax.ShapeDtypeStruct((DEC_BATCH, MOBA_HEADS, P, LANES), jnp.int32),
        compiler_params=_params("arbitrary"),
        name="sample_block_select",
    )(mq, kmean)


def _sample_attend(mq, mk, mv, cache_k, cache_v, page_table, blk_ids, moba_o):
    ppb = MOBA_BLOCK // PAGE_SIZE
    n_sel = MOBA_TOPK * MOBA_BLOCK
    nb = SEQS_PER_STEP
    n_q = nb * DEC_SEQ
    n_copies = n_q * MOBA_TOPK * ppb
    n_steps = (DEC_BATCH // nb) * MOBA_HEADS
    scale = MOBA_DH ** -0.5
    P = DEC_PAD
    row_blk0 = SEQ // (nb * P)

    def body(pt_ref, ids_ref, q_ref, kn_ref, vn_ref, ck_ref, cv_ref, _, o_ref, kbuf, vbuf, sem):
        step = pl.program_id(0) * MOBA_HEADS + pl.program_id(1)

        def copies(st, slot):
            g, h = st // MOBA_HEADS, st % MOBA_HEADS
            out = []
            for qi in range(n_q):
                b, t = g * nb + qi // DEC_SEQ, qi % DEC_SEQ
                for sl in range(MOBA_TOPK):
                    blk = ids_ref[((b * DEC_SEQ + t) * MOBA_HEADS + h) * MOBA_TOPK + sl]
                    for pg in range(ppb):
                        phys = pt_ref[b, blk * ppb + pg]
                        n = (qi * MOBA_TOPK + sl) * ppb + pg
                        rows = pl.ds((sl * ppb + pg) * PAGE_SIZE, PAGE_SIZE)
                        for src, dst, which in ((ck_ref, kbuf, 0), (cv_ref, vbuf, 1)):
                            out.append(pltpu.make_async_copy(src.at[phys, :, h, :], dst.at[slot, qi, rows, :],
                                                             sem.at[slot, which, n]))
            return out

        slot = step % 2

        @pl.when(step == 0)
        def _():
            for n, cp in enumerate(copies(step, 0)):
                cp.start(priority=n % 2)

        @pl.when(step + 1 < n_steps)
        def _():
            for n, cp in enumerate(copies(step + 1, 1 - slot)):
                cp.start(priority=n % 2)

        for cp in copies(step, slot):
            cp.wait()

        q, kn, vn = q_ref[...], kn_ref[...], vn_ref[...]
        rown = _iota((P, 1), 0)
        out_rows = []
        for bb in range(nb):
            knb, vnb = kn[bb * P:(bb + 1) * P], vn[bb * P:(bb + 1) * P]
            res = jnp.zeros((P, MOBA_DH), F32)
            for t in range(DEC_SEQ):
                qi = bb * DEC_SEQ + t
                qt = q[bb * P + t:bb * P + t + 1, :] * scale
                s = jnp.sum(kbuf[slot, qi] * qt, axis=-1, keepdims=True)
                sn = jnp.where(rown <= t, jnp.sum(knb * qt, axis=-1, keepdims=True), NEG)
                m = jnp.maximum(jnp.max(s, axis=0, keepdims=True), jnp.max(sn, axis=0, keepdims=True))
                p, pn = jnp.exp(s - m), jnp.exp(sn - m)
                l = jnp.sum(p, axis=0, keepdims=True) + jnp.sum(pn, axis=0, keepdims=True)
                o = (jnp.sum(p * vbuf[slot, qi], axis=0, keepdims=True)
                     + jnp.sum(pn * vnb, axis=0, keepdims=True))
                res = jnp.where(rown == t, o / l, res)
            out_rows.append(res)
        o_ref[...] = jnp.concatenate(out_rows, axis=0).astype(o_ref.dtype)

    head_blk = pl.BlockSpec((nb * P, MOBA_DH), lambda g, h, pt, ids: (row_blk0 + g, h))
    any_spec = pl.BlockSpec(memory_space=pl.ANY)
    grid_spec = pltpu.PrefetchScalarGridSpec(
        num_scalar_prefetch=2,
        grid=(DEC_BATCH // nb, MOBA_HEADS),
        in_specs=[head_blk, head_blk, head_blk, any_spec, any_spec, any_spec],
        out_specs=head_blk,
        scratch_shapes=[
            pltpu.VMEM((2, n_q, n_sel, MOBA_DH), F32),
            pltpu.VMEM((2, n_q, n_sel, MOBA_DH), F32),
            pltpu.SemaphoreType.DMA((2, 2, n_copies)),
        ],
    )
    return pl.pallas_call(
        body,
        grid_spec=grid_spec,
        out_shape=jax.ShapeDtypeStruct((ROWS, MOBA_HEADS * MOBA_DH), BF16),
        input_output_aliases={7: 0},
        compiler_params=_params("arbitrary", "arbitrary"),
        name="sample_attend",
    )(page_table, blk_ids, mq, mk, mv, cache_k, cache_v, moba_o)


def _softmax_attend(q, k, v):
    s = lax.dot_general(q, k, _NT, preferred_element_type=F32)
    m = jnp.max(s, axis=-1, keepdims=True)
    p = jnp.exp(s - m)
    l = jnp.sum(p, axis=-1, keepdims=True)
    return jnp.dot(p.astype(v.dtype), v, preferred_element_type=F32) / l


def _mem_attend_prompt(cq, mem_k, mem_v):
    tq = 512
    scale = MEM_DH ** -0.5

    def body(q_ref, k_ref, v_ref, _, o_ref):
        q = (q_ref[...] * scale).astype(BF16)
        o_ref[...] = _softmax_attend(q, k_ref[...].astype(BF16), v_ref[...].astype(BF16)).astype(o_ref.dtype)

    kv_spec = pl.BlockSpec((N_MEM, MEM_DH), lambda r, h: (0, h))
    return pl.pallas_call(
        body,
        grid=(SEQ // tq, MEM_HEADS),
        in_specs=[pl.BlockSpec((tq, MEM_DH), lambda r, h: (r, h)), kv_spec, kv_spec,
                  pl.BlockSpec(memory_space=pl.ANY)],
        out_specs=pl.BlockSpec((tq, MEM_DH), lambda r, h: (r, h)),
        out_shape=jax.ShapeDtypeStruct((ROWS, MEM_HEADS * MEM_DH), BF16),
        input_output_aliases={3: 0},
        compiler_params=_params("arbitrary", "arbitrary"),
        name="mem_attend_prompt",
    )(cq, mem_k, mem_v, _zero_slab(MEM_HEADS * MEM_DH))


def _mem_attend_sample(cq, mem_k, mem_v, mem_o):
    nb = SEQS_PER_STEP
    P = DEC_PAD
    w = MEM_HEADS * MEM_DH
    scale = MEM_DH ** -0.5
    row_blk0 = SEQ // (nb * P)

    def body(q_ref, k_ref, v_ref, _, o_ref):
        qa = q_ref[...] * scale
        out_rows = []
        for bb in range(nb):
            heads = []
            for h in range(MEM_HEADS):
                cols = slice(h * MEM_DH, (h + 1) * MEM_DH)
                heads.append(_softmax_attend(qa[bb * P:(bb + 1) * P, cols], k_ref[bb, :, h, :],
                                             v_ref[bb, :, h, :]))
            out_rows.append(jnp.concatenate(heads, axis=1))
        o_ref[...] = jnp.concatenate(out_rows, axis=0).astype(o_ref.dtype)

    kv_spec = pl.BlockSpec((nb, N_MEM, MEM_HEADS, MEM_DH), lambda g: (g, 0, 0, 0))
    row_spec = pl.BlockSpec((nb * P, w), lambda g: (row_blk0 + g, 0))
    return pl.pallas_call(
        body,
        grid=(DEC_BATCH // nb,),
        in_specs=[row_spec, kv_spec, kv_spec, pl.BlockSpec(memory_space=pl.ANY)],
        out_specs=row_spec,
        out_shape=jax.ShapeDtypeStruct((ROWS, w), BF16),
        input_output_aliases={3: 0},
        compiler_params=_params("arbitrary"),
        name="mem_attend_sample",
    )(cq, mem_k, mem_v, mem_o)


def kernel(x_prompt, x_sample, cache_k, cache_v, cache_mem_k, cache_mem_v, state_ret, page_table, mem_prompt, attn_norm_w, w_in, moba_q_norm, moba_k_norm, mem_q_norm, mem_k_norm, mem_norm_w, w_mem_kv, ret_gn_w, w_ret_o, w_moba_o, w_mem_o, w_out, mlp_norm_w, w_up, w_down):
    hd_mem = MEM_HEADS * MEM_DH

    xs = jnp.pad(x_sample, ((0, 0), (0, DEC_PAD - DEC_SEQ), (0, 0))).reshape(ROWS_S, D_MODEL)
    x, xn = _stack_rmsnorm(x_prompt[0], xs, attn_norm_w)

    half = RET_DK // 2
    inv = ROPE_BASE ** (-jnp.arange(half, dtype=F32) * (2.0 / RET_DK))
    inv = jnp.concatenate([inv, inv]).reshape(1, LANES)
    log_g = jnp.log1p(-jnp.exp2(-5.0 - jnp.arange(RET_HEADS, dtype=F32)))
    cos_t, sin_t = _rope_tables(inv)

    rope_args = [(cos_t, (TM, LANES), lambda i, j: (i, 0)), (sin_t, (TM, LANES), lambda i, j: (i, 0))]

    def norm_arg(wv):
        return [(wv.reshape(1, -1), (1, wv.shape[0]), lambda i, j: (0, 0))]

    def proj(col0, n_cols, epi, args, dt, name):
        return _matmul(xn, w_in, col0, n_cols, TM, TN, epi, args, [dt], name)

    rq = proj(C_RQ, 1024, functools.partial(_epi_rope, 1.0), rope_args, BF16, "proj_rq")
    rk = proj(C_RK, 1024, functools.partial(_epi_rope, RET_DK ** -0.5), rope_args, BF16, "proj_rk")
    rv = proj(C_RV, 2048, _epi_plain, [], BF16, "proj_rv")
    rg = proj(C_RG, 2048, _epi_plain, [], F32, "proj_rg")
    mq = proj(C_MQ, 1024, functools.partial(_epi_headnorm, MOBA_DH), norm_arg(moba_q_norm), F32, "proj_mq")
    mk = proj(C_MK, 1024, functools.partial(_epi_headnorm, MOBA_DH), norm_arg(moba_k_norm), F32, "proj_mk")
    mv = proj(C_MV, 1024, _epi_plain, [], F32, "proj_mv")
    cq = proj(C_CQ, 1024, functools.partial(_epi_headnorm, MEM_DH), norm_arg(mem_q_norm), F32, "proj_cq")
    gates = proj(C_GATE, 3 * D_MODEL, _epi_sigmoid, [], BF16, "proj_gates")

    mn = _rmsnorm_cast(mem_prompt[0], mem_norm_w, N_MEM)
    mem_k_p = _matmul(mn, w_mem_kv, 0, hd_mem, N_MEM, TN, functools.partial(_epi_headnorm, MEM_DH),
                      norm_arg(mem_k_norm), [F32], "mem_k")
    mem_v_p = _matmul(mn, w_mem_kv, hd_mem, hd_mem, N_MEM, TN, _epi_plain, [], [F32], "mem_v")

    gn_w = ret_gn_w.reshape(1, -1)
    ret_o, ret_state_p = _retention_prompt(log_g, rq, rk, rv, rg, gn_w)
    ret_o, ret_state_s = _retention_sample(log_g, rq, rk, rv, rg, gn_w, state_ret, ret_o)

    moba_o, kmean_s = _moba_prompt(mq, mk, mv, cache_k, page_table)
    ids = _sample_block_select(mq, kmean_s)
    ids = ids[:, :, :DEC_SEQ, :MOBA_TOPK].transpose(0, 2, 1, 3).reshape(-1)
    moba_o = _sample_attend(mq, mk, mv, cache_k, cache_v, page_table, ids, moba_o)

    mem_o = _mem_attend_prompt(cq, mem_k_p, mem_v_p)
    mem_o = _mem_attend_sample(cq, cache_mem_k, cache_mem_v, mem_o)

    tn = TN_EPI
    merged = _gated_branch_merge([ret_o, moba_o, mem_o],
                                 [w_ret_o.astype(BF16), w_moba_o.astype(BF16), w_mem_o.astype(BF16)], gates)
    hres = _matmul(merged, w_out, 0, D_MODEL, TM, tn, _epi_residual,
                   [(x, (TM, tn), lambda i, j: (i, j))], [F32], "out_proj")
    hn = _rmsnorm_cast(hres, mlp_norm_w, 384)
    u2 = _matmul(hn, w_up, 0, D_FF, TM, TN, _epi_relu_sq, [], [BF16], "mlp_up")
    tm_down = TM // 2
    y = _matmul(u2, w_down.astype(BF16), 0, D_MODEL, tm_down, tn, _epi_residual,
                [(hres, (tm_down, tn), lambda i, j: (i, j))], [F32], "mlp_down")

    def sample_rows(a):
        return a.reshape(DEC_BATCH, DEC_PAD, *a.shape[1:])[:, :DEC_SEQ]

    shape4 = (MOBA_HEADS, MOBA_DH)
    return (
        y[:SEQ][None],
        sample_rows(y[SEQ:]),
        mk[:SEQ].reshape(1, SEQ, *shape4),
        mv[:SEQ].reshape(1, SEQ, *shape4),
        ret_state_p[None],
        mem_k_p.reshape(1, N_MEM, MEM_HEADS, MEM_DH),
        mem_v_p.reshape(1, N_MEM, MEM_HEADS, MEM_DH),
        sample_rows(mk[SEQ:]).reshape(DEC_BATCH, DEC_SEQ, *shape4),
        sample_rows(mv[SEQ:]).reshape(DEC_BATCH, DEC_SEQ, *shape4),
        ret_state_s,
    )
```

```python
import functools

import jax
import jax.numpy as jnp
from jax import lax
from jax.experimental import pallas as pl
from jax.experimental.pallas import tpu as pltpu

F32 = jnp.float32
BF16 = jnp.bfloat16

D_MODEL = 2048
SEQ = 8192
DEC_BATCH = 32
DEC_SEQ = 4
DEC_PAD = 8
PAST_LEN = 16384
PAGE_SIZE = 128
N_MEM = 256
RET_HEADS = 8
RET_DK = 128
RET_DV = 256
RET_CHUNK = 128
ROPE_BASE = 10000.0
MOBA_HEADS = 8
MOBA_DH = 128
MOBA_BLOCK = 256
MOBA_TOPK = 3
MEM_HEADS = 4
MEM_DH = 256
D_FF = 4 * D_MODEL
EPS = 1e-6
GN_EPS = 1e-5
NEG = -1e30
LOG2_E = 1.4426950408889634

ROWS_S = DEC_BATCH * DEC_PAD
ROWS = SEQ + ROWS_S
TM = 1056
TN = 1024
TN_EPI = 512
LANES = 128
SEQS_PER_STEP = 2
VMEM_LIMIT = 56 * 1024 * 1024

C_RQ, C_RK, C_RV, C_RG = 0, 1024, 2048, 4096
C_MQ, C_MK, C_MV, C_CQ, C_GATE = 6144, 7168, 8192, 9216, 10240

_NT = (((1,), (1,)), ((), ()))


def _iota(shape, dim):
    return lax.broadcasted_iota(jnp.int32, shape, dim)


def _params(*sem):
    return pltpu.CompilerParams(dimension_semantics=sem, vmem_limit_bytes=VMEM_LIMIT)


def _zero_slab(width):
    return jnp.zeros((ROWS, width), BF16)


def _rmsnorm_cast(x, w, tm):
    rows, d = x.shape

    def body(x_ref, w_ref, o_ref):
        xf = x_ref[...]
        y = xf * lax.rsqrt(jnp.mean(xf * xf, axis=-1, keepdims=True) + EPS)
        o_ref[...] = (y * w_ref[...]).astype(o_ref.dtype)

    return pl.pallas_call(
        body,
        grid=(rows // tm,),
        in_specs=[pl.BlockSpec((tm, d), lambda i: (i, 0)), pl.BlockSpec((1, d), lambda i: (0, 0))],
        out_specs=pl.BlockSpec((tm, d), lambda i: (i, 0)),
        out_shape=jax.ShapeDtypeStruct((rows, d), BF16),
        compiler_params=_params("arbitrary"),
        name="rmsnorm_cast",
    )(x, w.reshape(1, d))


def _stack_rmsnorm(xp, xs, w):
    tm = ROWS_S
    n_p = SEQ // tm
    d = xp.shape[1]

    def body(xp_ref, xs_ref, w_ref, x_ref, o_ref):
        xf = jnp.where(pl.program_id(0) < n_p, xp_ref[...], xs_ref[...])
        x_ref[...] = xf
        y = xf * lax.rsqrt(jnp.mean(xf * xf, axis=-1, keepdims=True) + EPS)
        o_ref[...] = (y * w_ref[...]).astype(o_ref.dtype)

    return pl.pallas_call(
        body,
        grid=(n_p + 1,),
        in_specs=[pl.BlockSpec((tm, d), lambda i: (jnp.minimum(i, n_p - 1), 0)),
                  pl.BlockSpec((tm, d), lambda i: (0, 0)),
                  pl.BlockSpec((1, d), lambda i: (0, 0))],
        out_specs=[pl.BlockSpec((tm, d), lambda i: (i, 0))] * 2,
        out_shape=[jax.ShapeDtypeStruct((ROWS, d), F32), jax.ShapeDtypeStruct((ROWS, d), BF16)],
        compiler_params=_params("arbitrary"),
        name="stack_rmsnorm",
    )(xp, xs, w.reshape(1, d))


def _rope_tables(inv):
    tm = TM

    def body(inv_ref, c_ref, s_ref):
        row = pl.program_id(0) * tm + _iota((tm, LANES), 0)
        pos = jnp.where(row < SEQ, row, PAST_LEN + ((row - SEQ) & (DEC_PAD - 1)))
        ang = pos.astype(F32) * inv_ref[...]
        sn = jnp.sin(ang)
        c_ref[...] = jnp.cos(ang)
        s_ref[...] = jnp.where(_iota((tm, LANES), 1) < RET_DK // 2, -sn, sn)

    return pl.pallas_call(
        body,
        grid=(ROWS // tm,),
        in_specs=[pl.BlockSpec((1, LANES), lambda i: (0, 0))],
        out_specs=[pl.BlockSpec((tm, LANES), lambda i: (i, 0))] * 2,
        out_shape=[jax.ShapeDtypeStruct((ROWS, LANES), F32)] * 2,
        compiler_params=_params("arbitrary"),
        name="rope_tables",
    )(inv)


def _matmul(a, w, col0, n_cols, tm, tn, epi_fn, epi_args, out_dtypes, name):
    m, k = a.shape
    assert m % tm == 0 and n_cols % tn == 0 and col0 % tn == 0
    jb0 = col0 // tn
    n_epi, n_out = len(epi_args), len(out_dtypes)
    cast_once = w.dtype != BF16

    def body(a_ref, w_ref, *rest):
        epi_refs, out_refs = rest[:n_epi], rest[n_epi:n_epi + n_out]
        if cast_once:
            wb_ref = rest[-1]

            @pl.when(pl.program_id(1) == 0)
            def _():
                wb_ref[...] = w_ref[...].astype(BF16)
        else:
            wb_ref = w_ref
        acc = jnp.dot(a_ref[...], wb_ref[...], preferred_element_type=F32)
        for o_ref, o in zip(out_refs, epi_fn(acc, *epi_refs)):
            o_ref[...] = o.astype(o_ref.dtype)

    def spec(block, index_map):
        if cast_once:
            return pl.BlockSpec(block, lambda j, i: index_map(i, j))
        return pl.BlockSpec(block, index_map)

    in_specs = [spec((tm, k), lambda i, j: (i, 0)), spec((k, tn), lambda i, j: (0, jb0 + j))]
    in_specs += [spec(bs, im) for _, bs, im in epi_args]
    n_i, n_j = m // tm, n_cols // tn
    outs = pl.pallas_call(
        body,
        grid=(n_j, n_i) if cast_once else (n_i, n_j),
        in_specs=in_specs,
        out_specs=[spec((tm, tn), lambda i, j: (i, j)) for _ in out_dtypes],
        out_shape=[jax.ShapeDtypeStruct((m, n_cols), dt) for dt in out_dtypes],
        scratch_shapes=[pltpu.VMEM((k, tn), BF16)] if cast_once else [],
        compiler_params=_params("arbitrary", "arbitrary"),
        name=name,
    )(a, w, *[x for x, _, _ in epi_args])
    return outs[0] if n_out == 1 else outs


def _epi_plain(acc):
    return (acc,)


def _epi_sigmoid(acc):
    return (1.0 / (1.0 + jnp.exp(-acc)),)


def _epi_relu_sq(acc):
    u = jnp.maximum(acc, 0.0)
    return (u * u,)


def _epi_rope(scale, acc, c_ref, s_ref):
    c, s = c_ref[...], s_ref[...]
    pieces = []
    for hh in range(acc.shape[1] // RET_DK):
        x = acc[:, hh * RET_DK:(hh + 1) * RET_DK]
        y = x * c + pltpu.roll(x, RET_DK // 2, 1) * s
        pieces.append(y * scale if scale != 1.0 else y)
    return (jnp.concatenate(pieces, axis=1),)


def _epi_headnorm(hw, acc, w_ref):
    w = w_ref[...]
    pieces = []
    for hh in range(acc.shape[1] // hw):
        x = acc[:, hh * hw:(hh + 1) * hw]
        pieces.append(x * lax.rsqrt(jnp.mean(x * x, axis=-1, keepdims=True) + EPS) * w)
    return (jnp.concatenate(pieces, axis=1),)


def _epi_residual(acc, x_ref):
    return (x_ref[...] + acc,)


def _gated_branch_merge(branches, weights, gates):
    n_b = len(branches)
    tm, tn = TM, TN_EPI

    def body(*refs):
        a_refs, w_refs, g_refs, o_ref = refs[:n_b], refs[n_b:2 * n_b], refs[2 * n_b:3 * n_b], refs[3 * n_b]
        acc = None
        for a_ref, w_ref, g_ref in zip(a_refs, w_refs, g_refs):
            term = g_ref[...].astype(F32) * jnp.dot(a_ref[...], w_ref[...], preferred_element_type=F32)
            acc = term if acc is None else acc + term
        o_ref[...] = acc.astype(o_ref.dtype)

    in_specs = [pl.BlockSpec((tm, a.shape[1]), lambda i, j: (i, 0)) for a in branches]
    in_specs += [pl.BlockSpec((w.shape[0], tn), lambda i, j: (0, j)) for w in weights]
    in_specs += [pl.BlockSpec((tm, tn), lambda i, j, b=b: (i, b * (D_MODEL // tn) + j)) for b in range(n_b)]
    return pl.pallas_call(
        body,
        grid=(ROWS // tm, D_MODEL // tn),
        in_specs=in_specs,
        out_specs=pl.BlockSpec((tm, tn), lambda i, j: (i, j)),
        out_shape=jax.ShapeDtypeStruct((ROWS, D_MODEL), BF16),
        compiler_params=_params("arbitrary", "arbitrary"),
        name="branch_merge",
    )(*branches, *weights, *([gates] * n_b))


def _gated_groupnorm(o, rg, gw):
    mu = jnp.mean(o, axis=-1, keepdims=True)
    d = o - mu
    var = jnp.mean(d * d, axis=-1, keepdims=True)
    rn = d * lax.rsqrt(var + GN_EPS) * gw
    return rg / (1.0 + jnp.exp(-rg)) * rn


def _retention_prompt(log_g, rq, rk, rv, rg, gn_w):
    C = RET_CHUNK
    nc = SEQ // C
    HP = 4

    def body(lg_ref, q_ref, k_ref, v_ref, rg_ref, gw_ref, _, o_ref, st_ref, state, dmask_t, qdec_t, kdec_t):
        hp, c = pl.program_id(0), pl.program_id(1)

        @pl.when(c == 0)
        def _():
            state[...] = jnp.zeros_like(state)
            diff = (_iota((C, C), 0) - _iota((C, C), 1)).astype(F32)
            ii = _iota((C, 1), 0).astype(F32)
            for hh in range(HP):
                lg = lg_ref[hp * HP + hh]
                dmask_t[hh] = jnp.where(diff >= 0, jnp.exp(jnp.maximum(diff, 0.0) * lg), 0.0)
                qdec_t[hh] = jnp.broadcast_to(jnp.exp((ii + 1.0) * lg), (C, RET_DV))
                kdec_t[hh] = jnp.broadcast_to(jnp.exp((C - 1.0 - ii) * lg), (C, RET_DK))

        for hh in range(HP):
            lg = lg_ref[hp * HP + hh]
            q = q_ref[:, hh * RET_DK:(hh + 1) * RET_DK]
            k = k_ref[:, hh * RET_DK:(hh + 1) * RET_DK]
            v = v_ref[:, hh * RET_DV:(hh + 1) * RET_DV]
            s = lax.dot_general(q, k, _NT, preferred_element_type=F32) * dmask_t[hh]
            c_decay = jnp.exp(jnp.zeros((1, RET_DV), F32) + C * lg)
            s_old = state[hh]
            o = (jnp.dot(s.astype(BF16), v, preferred_element_type=F32)
                 + jnp.dot(q, s_old.astype(BF16), preferred_element_type=F32) * qdec_t[hh])
            kd_t = (k.astype(F32) * kdec_t[hh]).T.astype(BF16)
            s_new = s_old * c_decay + jnp.dot(kd_t, v, preferred_element_type=F32)
            state[hh] = s_new
            cols = slice(hh * RET_DV, (hh + 1) * RET_DV)
            o_ref[:, cols] = _gated_groupnorm(o, rg_ref[:, cols], gw_ref[:, cols]).astype(o_ref.dtype)

        @pl.when(c == nc - 1)
        def _():
            st_ref[...] = state[...]

    return pl.pallas_call(
        body,
        grid=(RET_HEADS // HP, nc),
        in_specs=[
            pl.BlockSpec(memory_space=pltpu.SMEM),
            pl.BlockSpec((C, HP * RET_DK), lambda h, c: (c, h)),
            pl.BlockSpec((C, HP * RET_DK), lambda h, c: (c, h)),
            pl.BlockSpec((C, HP * RET_DV), lambda h, c: (c, h)),
            pl.BlockSpec((C, HP * RET_DV), lambda h, c: (c, h)),
            pl.BlockSpec((1, HP * RET_DV), lambda h, c: (0, h)),
            pl.BlockSpec(memory_space=pl.ANY),
        ],
        out_specs=[
            pl.BlockSpec((C, HP * RET_DV), lambda h, c: (c, h)),
            pl.BlockSpec((HP, RET_DK, RET_DV), lambda h, c: (h, 0, 0)),
        ],
        out_shape=[
            jax.ShapeDtypeStruct((ROWS, RET_HEADS * RET_DV), BF16),
            jax.ShapeDtypeStruct((RET_HEADS, RET_DK, RET_DV), F32),
        ],
        input_output_aliases={6: 0},
        scratch_shapes=[
            pltpu.VMEM((HP, RET_DK, RET_DV), F32),
            pltpu.VMEM((HP, C, C), F32),
            pltpu.VMEM((HP, C, RET_DV), F32),
            pltpu.VMEM((HP, C, RET_DK), F32),
        ],
        compiler_params=_params("arbitrary", "arbitrary"),
        name="retention_prompt",
    )(log_g, rq, rk, rv, rg, gn_w, _zero_slab(RET_HEADS * RET_DV))


def _retention_sample(log_g, rq, rk, rv, rg, gn_w, state_ret, ret_o):
    T = DEC_SEQ
    P = DEC_PAD
    nb = SEQS_PER_STEP
    row_blk0 = SEQ // (nb * P)

    def body(lg_ref, q_ref, k_ref, v_ref, rg_ref, gw_ref, st_in_ref, _, o_ref, st_ref):
        qa, ka, va = q_ref[...].astype(F32), k_ref[...].astype(F32), v_ref[...].astype(F32)
        rga, gw = rg_ref[...], gw_ref[...]
        ri, ci = _iota((P, P), 0), _iota((P, P), 1)
        diff = (ri - ci).astype(F32)
        ii = _iota((P, 1), 0)
        iif = ii.astype(F32)
        live = ii < T
        out_rows = []
        for bb in range(nb):
            rows = slice(bb * P, (bb + 1) * P)
            out_heads = []
            for h in range(RET_HEADS):
                lg = lg_ref[h]
                q = qa[rows, h * RET_DK:(h + 1) * RET_DK]
                k = jnp.where(live, ka[rows, h * RET_DK:(h + 1) * RET_DK], 0.0)
                v = jnp.where(live, va[rows, h * RET_DV:(h + 1) * RET_DV], 0.0)
                dmask = jnp.where(diff >= 0, jnp.exp(jnp.maximum(diff, 0.0) * lg), 0.0)
                s = lax.dot_general(q, k, _NT, preferred_element_type=F32) * dmask
                q_decay = jnp.exp((iif + 1.0) * lg)
                k_decay = jnp.exp((T - 1.0 - iif) * lg)
                c_decay = jnp.exp(jnp.zeros((1, RET_DV), F32) + T * lg)
                s_old = st_in_ref[bb, h]
                o = (jnp.dot(s, v, preferred_element_type=F32)
                     + jnp.dot(q, s_old, preferred_element_type=F32) * q_decay)
                upd = lax.dot_general(k * k_decay, v, (((0,), (0,)), ((), ())),
                                      preferred_element_type=F32)
                st_ref[bb, h] = s_old * c_decay + upd
                out_heads.append(_gated_groupnorm(
                    o, rga[rows, h * RET_DV:(h + 1) * RET_DV], gw[:, h * RET_DV:(h + 1) * RET_DV]))
            out_rows.append(jnp.concatenate(out_heads, axis=1))
        o_ref[...] = jnp.concatenate(out_rows, axis=0).astype(o_ref.dtype)

    wq, wv = RET_HEADS * RET_DK, RET_HEADS * RET_DV
    return pl.pallas_call(
        body,
        grid=(DEC_BATCH // nb,),
        in_specs=[
            pl.BlockSpec(memory_space=pltpu.SMEM),
            pl.BlockSpec((nb * P, wq), lambda g: (row_blk0 + g, 0)),
            pl.BlockSpec((nb * P, wq), lambda g: (row_blk0 + g, 0)),
            pl.BlockSpec((nb * P, wv), lambda g: (row_blk0 + g, 0)),
            pl.BlockSpec((nb * P, wv), lambda g: (row_blk0 + g, 0)),
            pl.BlockSpec((1, wv), lambda g: (0, 0)),
            pl.BlockSpec((nb, RET_HEADS, RET_DK, RET_DV), lambda g: (g, 0, 0, 0)),
            pl.BlockSpec(memory_space=pl.ANY),
        ],
        out_specs=[
            pl.BlockSpec((nb * P, wv), lambda g: (row_blk0 + g, 0)),
            pl.BlockSpec((nb, RET_HEADS, RET_DK, RET_DV), lambda g: (g, 0, 0, 0)),
        ],
        out_shape=[
            jax.ShapeDtypeStruct((ROWS, wv), BF16),
            jax.ShapeDtypeStruct((DEC_BATCH, RET_HEADS, RET_DK, RET_DV), F32),
        ],
        input_output_aliases={7: 0},
        compiler_params=_params("arbitrary"),
        name="retention_sample",
    )(log_g, rq, rk, rv, rg, gn_w, state_ret, ret_o)


def _moba_prompt(mq, mk, mv, cache_k, page_table):
    BL = MOBA_BLOCK
    nblk = SEQ // BL
    TQ = 2 * BL
    scale = MOBA_DH ** -0.5

    n_sblk = PAST_LEN // MOBA_BLOCK
    ppb = MOBA_BLOCK // PAGE_SIZE
    UNITS = DEC_BATCH * n_sblk
    n_tiles = SEQ // TQ
    trips_per_head = (n_tiles // 2) ** 2
    U_TILE, U_TRIP = 4, 3
    assert MOBA_HEADS * (n_tiles * U_TILE + trips_per_head * U_TRIP) == UNITS
    AHEAD = 12
    RING = AHEAD + max(U_TILE, U_TRIP)
    assert RING & (RING - 1) == 0 and UNITS & (UNITS - 1) == 0 and n_sblk & (n_sblk - 1) == 0
    sblk_shift = n_sblk.bit_length() - 1

    def body(pt_ref, q_ref, k_ref, v_ref, _, ck_ref, o_ref, km_ref, kaug, vt, kmean, s_a, s_b, p_a, p_b,
             ring, sem):
        h, ti = pl.program_id(0), pl.program_id(1)

        def unit_copies(u):
            uu = u & (UNITS - 1)
            b, j = uu >> sblk_shift, uu & (n_sblk - 1)
            slot = u & (RING - 1)
            return [pltpu.make_async_copy(ck_ref.at[pt_ref[b, j * ppb + pg]], ring.at[slot, pg],
                                          sem.at[slot, pg]) for pg in range(ppb)]

        def fetch(u0, n):
            for i in range(n):
                for cp in unit_copies(u0 + AHEAD + i):
                    cp.start()

        def reduce(u0, n):
            for i in range(n):
                u = u0 + i
                for cp in unit_copies(u):
                    cp.wait()
                slot = u & (RING - 1)
                lanes8 = (8, PAGE_SIZE // 8, MOBA_HEADS, MOBA_DH)
                parts = [jnp.sum(ring[slot, pg].reshape(lanes8), axis=1) for pg in range(ppb)]
                tot = jnp.sum(functools.reduce(lambda a, b: a + b, parts), axis=0)
                km_ref[u >> sblk_shift, u & (n_sblk - 1)] = tot * (1.0 / MOBA_BLOCK)

        @pl.when(jnp.logical_and(h == 0, ti == 0))
        def _():
            fetch(-AHEAD, AHEAD)

        u_tile = (h * n_tiles + ti) * U_TILE + (h * trips_per_head + ((ti * ti) >> 2)) * U_TRIP
        fetch(u_tile, U_TILE)
        reduce(u_tile, U_TILE)

        @pl.when(ti == 0)
        def _():
            kmean[...] = jnp.zeros_like(kmean)
            lane = _iota((BL, LANES), 1)

            def prep(jb, carry):
                r0 = pl.multiple_of(jb * BL, BL)
                kb = k_ref[pl.ds(r0, BL), :]
                kmean[pl.ds(jb, 1), :] = jnp.mean(kb, axis=0, keepdims=True)
                kaug[pl.ds(r0, BL), 0:MOBA_DH] = kb.astype(BF16)
                kaug[pl.ds(r0, BL), MOBA_DH:2 * MOBA_DH] = (lane == jb).astype(BF16)
                vt[:, pl.ds(r0, BL)] = v_ref[pl.ds(r0, BL), :].T.astype(BF16)
                return carry

            lax.fori_loop(0, nblk, prep, 0)

        q = q_ref[...]
        gt = lax.dot_general(kmean[0:nblk, :], q, _NT, precision=lax.Precision.HIGHEST,
                             preferred_element_type=F32)
        n_sg = nblk // 8
        row = _iota((nblk, TQ), 0)
        row8 = _iota((8, TQ), 0)
        second = jnp.where(_iota((8, TQ), 1) >= BL, 1, 0)
        own = 2 * ti + second
        gtv = jnp.where(row < 2 * ti + jnp.where(_iota((nblk, TQ), 1) >= BL, 1, 0), gt, -jnp.inf)
        cnts = [jnp.zeros((8, TQ), jnp.int32) for _ in range(n_sg)]
        for blk in range(nblk):
            g_blk = gtv[blk:blk + 1, :]
            for sg in range(n_sg):
                g_sg = gt[8 * sg:8 * sg + 8, :]
                if 8 * sg > blk:
                    cnts[sg] = cnts[sg] + jnp.where(g_blk >= g_sg, 1, 0)
                elif 8 * sg + 7 < blk:
                    cnts[sg] = cnts[sg] + jnp.where(g_blk > g_sg, 1, 0)
                else:
                    cnts[sg] = cnts[sg] + jnp.where(row8 + 8 * sg > blk, jnp.where(g_blk >= g_sg, 1, 0),
                                                    jnp.where(g_blk > g_sg, 1, 0))
        picked = [(cnts[sg] < MOBA_TOPK) & (row8 + 8 * sg < own) for sg in range(n_sg)]
        qs = (q * (scale * LOG2_E)).astype(BF16)

        parts = [jnp.where(kp, 0.0, NEG) for kp in picked] + [jnp.zeros((LANES - nblk, TQ), F32)]
        bias = jnp.concatenate(parts, axis=0).T
        blk_lane = _iota((TQ, LANES), 1)
        own_row = 2 * ti + jnp.where(_iota((TQ, LANES), 0) >= BL, 1, 0)

        def with_bias(b):
            return jnp.concatenate([qs, b.astype(BF16)], axis=1)

        qa_past = with_bias(jnp.where(blk_lane < 2 * ti, bias, NEG))
        qa_own = with_bias(jnp.where(blk_lane == own_row, 0.0, bias))

        def scores(qx, g):
            c0 = pl.multiple_of(g * TQ, TQ)
            return lax.dot_general(kaug[pl.ds(c0, TQ), :], qx, _NT, preferred_element_type=F32)

        def values(g):
            c0 = pl.multiple_of(g * TQ, TQ)
            return vt[:, pl.ds(c0, TQ)]

        CH = 64

        def fold8(x, fn):
            return fn(x.reshape(CH // 8, 8, TQ), axis=0)

        def softmax_stage(s_ref, p_ref, m, l):
            mx8 = jnp.broadcast_to(m, (8, TQ))
            for c in range(TQ // CH):
                mx8 = jnp.maximum(mx8, fold8(s_ref[c * CH:(c + 1) * CH, :], jnp.max))
            m_new = jnp.max(mx8, axis=0, keepdims=True)
            alpha = jnp.exp2(m - m_new)
            sum8 = jnp.zeros((8, TQ), F32)
            for c in range(TQ // CH):
                pc = jnp.exp2(s_ref[c * CH:(c + 1) * CH, :] - m_new)
                sum8 = sum8 + fold8(pc, jnp.sum)
                p_ref[c * CH:(c + 1) * CH, :] = pc.astype(BF16)
            return m_new, l * alpha + jnp.sum(sum8, axis=0, keepdims=True), alpha

        last = SEQ // TQ - 1

        s_a[...] = jnp.where(_iota((TQ, TQ), 0) <= _iota((TQ, TQ), 1), scores(qa_own, ti), NEG)
        m, l, _ = softmax_stage(s_a, p_a, jnp.full((1, TQ), NEG, F32), jnp.zeros((1, TQ), F32))
        s_b[...] = scores(qa_past, 0)
        acc = jnp.zeros((MOBA_DH, TQ), F32)

        def trip(k, carry):
            m, l, acc = carry
            g0 = 2 * k
            u_trip = u_tile + U_TILE + k * U_TRIP
            fetch(u_trip, U_TRIP)
            s_a[...] = scores(qa_past, jnp.minimum(g0 + 1, last))
            pv = jnp.dot(values(jnp.where(k == 0, ti, g0 - 1)), p_a[...], preferred_element_type=F32)
            m, l, alpha = softmax_stage(s_b, p_b, m, l)
            acc = (acc + pv) * alpha
            s_b[...] = scores(qa_past, jnp.minimum(g0 + 2, last))
            pv = jnp.dot(values(g0), p_b[...], preferred_element_type=F32)
            m, l, alpha = softmax_stage(s_a, p_a, m, l)
            reduce(u_trip, U_TRIP)
            return m, l, (acc + pv) * alpha

        trips = (ti + 1) >> 1
        m, l, acc = lax.fori_loop(0, trips, trip, (m, l, acc))
        acc = acc + jnp.dot(values(jnp.where(trips == 0, ti, 2 * trips - 1)), p_a[...],
                            preferred_element_type=F32)
        o_ref[...] = (acc / l).T.astype(o_ref.dtype)

        @pl.when(jnp.logical_and(h == MOBA_HEADS - 1, ti == n_tiles - 1))
        def _():
            for u in range(UNITS, UNITS + AHEAD):
                for cp in unit_copies(u):
                    cp.wait()

    grid_spec = pltpu.PrefetchScalarGridSpec(
        num_scalar_prefetch=1,
        grid=(MOBA_HEADS, n_tiles),
        in_specs=[
            pl.BlockSpec((TQ, MOBA_DH), lambda h, i, pt: (i, h)),
            pl.BlockSpec((SEQ, MOBA_DH), lambda h, i, pt: (0, h)),
            pl.BlockSpec((SEQ, MOBA_DH), lambda h, i, pt: (0, h)),
            pl.BlockSpec(memory_space=pl.ANY),
            pl.BlockSpec(memory_space=pl.ANY),
        ],
        out_specs=[
            pl.BlockSpec((TQ, MOBA_DH), lambda h, i, pt: (i, h)),
            pl.BlockSpec((DEC_BATCH, n_sblk, MOBA_HEADS, MOBA_DH), lambda h, i, pt: (0, 0, 0, 0)),
        ],
        scratch_shapes=[
            pltpu.VMEM((SEQ, 2 * MOBA_DH), BF16),
            pltpu.VMEM((MOBA_DH, SEQ), BF16),
            pltpu.VMEM((LANES, MOBA_DH), F32),
            pltpu.VMEM((TQ, TQ), F32),
            pltpu.VMEM((TQ, TQ), F32),
            pltpu.VMEM((TQ, TQ), BF16),
            pltpu.VMEM((TQ, TQ), BF16),
            pltpu.VMEM((RING, ppb, PAGE_SIZE, MOBA_HEADS, MOBA_DH), F32),
            pltpu.SemaphoreType.DMA((RING, ppb)),
        ],
    )
    return pl.pallas_call(
        body,
        grid_spec=grid_spec,
        out_shape=[
            jax.ShapeDtypeStruct((ROWS, MOBA_HEADS * MOBA_DH), BF16),
            jax.ShapeDtypeStruct((DEC_BATCH, n_sblk, MOBA_HEADS, MOBA_DH), F32),
        ],
        input_output_aliases={4: 0},
        compiler_params=_params("arbitrary", "arbitrary"),
        name="moba_prompt",
    )(page_table, mq, mk, mv, _zero_slab(MOBA_HEADS * MOBA_DH), cache_k)


def _sample_block_select(mq, kmean):
    nblk = PAST_LEN // MOBA_BLOCK
    P = DEC_PAD
    row_blk0 = SEQ // P

    def body(q_ref, km_ref, o_ref):
        q = q_ref[...]
        lane = _iota((P, LANES), 1)
        for h in range(MOBA_HEADS):
            cols = slice(h * MOBA_DH, (h + 1) * MOBA_DH)
            kmh = jnp.concatenate([km_ref[0, :, h, :], jnp.zeros((LANES - nblk, MOBA_DH), F32)], axis=0)
            gate = lax.dot_general(q[:, cols], kmh, _NT, precision=lax.Precision.HIGHEST,
                                   preferred_element_type=F32)
            g = jnp.where(lane < nblk, gate, -jnp.inf)
            ids = []
            for _ in range(MOBA_TOPK):
                mx = jnp.max(g, axis=-1, keepdims=True)
                first = jnp.min(jnp.where(g == mx, lane, 1 << 20), axis=-1, keepdims=True)
                ids.append(first)
                g = jnp.where(lane == first, -jnp.inf, g)
            o_ref[0, h] = jnp.where(lane == 0, ids[0], jnp.where(lane == 1, ids[1], ids[2]))

    return pl.pallas_call(
        body,
        grid=(DEC_BATCH,),
        in_specs=[
            pl.BlockSpec((P, MOBA_HEADS * MOBA_DH), lambda b: (row_blk0 + b, 0)),
            pl.BlockSpec((1, nblk, MOBA_HEADS, MOBA_DH), lambda b: (b, 0, 0, 0)),
        ],
        out_specs=pl.BlockSpec((1, MOBA_HEADS, P, LANES), lambda b: (b, 0, 0, 0)),
        out_shape=jax.ShapeDtypeStruct((DEC_BATCH, MOBA_HEADS, P, LANES), jnp.int32),
        compiler_params=_params("arbitrary"),
        name="sample_block_select",
    )(mq, kmean)


def _sample_attend(mq, mk, mv, cache_k, cache_v, page_table, blk_ids, moba_o):
    ppb = MOBA_BLOCK // PAGE_SIZE
    n_sel = MOBA_TOPK * MOBA_BLOCK
    nb = SEQS_PER_STEP
    n_q = nb * DEC_SEQ
    n_copies = n_q * MOBA_TOPK * ppb
    n_steps = (DEC_BATCH // nb) * MOBA_HEADS
    scale = MOBA_DH ** -0.5
    P = DEC_PAD
    row_blk0 = SEQ // (nb * P)

    def body(pt_ref, ids_ref, q_ref, kn_ref, vn_ref, ck_ref, cv_ref, _, o_ref, kbuf, vbuf, sem):
        step = pl.program_id(0) * MOBA_HEADS + pl.program_id(1)

        def copies(st, slot):
            g, h = st // MOBA_HEADS, st % MOBA_HEADS
            out = []
            for qi in range(n_q):
                b, t = g * nb + qi // DEC_SEQ, qi % DEC_SEQ
                for sl in range(MOBA_TOPK):
                    blk = ids_ref[((b * DEC_SEQ + t) * MOBA_HEADS + h) * MOBA_TOPK + sl]
                    for pg in range(ppb):
                        phys = pt_ref[b, blk * ppb + pg]
                        n = (qi * MOBA_TOPK + sl) * ppb + pg
                        rows = pl.ds((sl * ppb + pg) * PAGE_SIZE, PAGE_SIZE)
                        for src, dst, which in ((ck_ref, kbuf, 0), (cv_ref, vbuf, 1)):
                            out.append(pltpu.make_async_copy(src.at[phys, :, h, :], dst.at[slot, qi, rows, :],
                                                             sem.at[slot, which, n]))
            return out

        slot = step % 2

        @pl.when(step == 0)
        def _():
            for n, cp in enumerate(copies(step, 0)):
                cp.start(priority=n % 2)

        @pl.when(step + 1 < n_steps)
        def _():
            for n, cp in enumerate(copies(step + 1, 1 - slot)):
                cp.start(priority=n % 2)

        for cp in copies(step, slot):
            cp.wait()

        q, kn, vn = q_ref[...], kn_ref[...], vn_ref[...]
        rown = _iota((P, 1), 0)
        out_rows = []
        for bb in range(nb):
            knb, vnb = kn[bb * P:(bb + 1) * P], vn[bb * P:(bb + 1) * P]
            res = jnp.zeros((P, MOBA_DH), F32)
            for t in range(DEC_SEQ):
                qi = bb * DEC_SEQ + t
                qt = q[bb * P + t:bb * P + t + 1, :] * scale
                s = jnp.sum(kbuf[slot, qi] * qt, axis=-1, keepdims=True)
                sn = jnp.where(rown <= t, jnp.sum(knb * qt, axis=-1, keepdims=True), NEG)
                m = jnp.maximum(jnp.max(s, axis=0, keepdims=True), jnp.max(sn, axis=0, keepdims=True))
                p, pn = jnp.exp(s - m), jnp.exp(sn - m)
                l = jnp.sum(p, axis=0, keepdims=True) + jnp.sum(pn, axis=0, keepdims=True)
                o = (jnp.sum(p * vbuf[slot, qi], axis=0, keepdims=True)
                     + jnp.sum(pn * vnb, axis=0, keepdims=True))
                res = jnp.where(rown == t, o / l, res)
            out_rows.append(res)
        o_ref[...] = jnp.concatenate(out_rows, axis=0).astype(o_ref.dtype)

    head_blk = pl.BlockSpec((nb * P, MOBA_DH), lambda g, h, pt, ids: (row_blk0 + g, h))
    any_spec = pl.BlockSpec(memory_space=pl.ANY)
    grid_spec = pltpu.PrefetchScalarGridSpec(
        num_scalar_prefetch=2,
        grid=(DEC_BATCH // nb, MOBA_HEADS),
        in_specs=[head_blk, head_blk, head_blk, any_spec, any_spec, any_spec],
        out_specs=head_blk,
        scratch_shapes=[
            pltpu.VMEM((2, n_q, n_sel, MOBA_DH), F32),
            pltpu.VMEM((2, n_q, n_sel, MOBA_DH), F32),
            pltpu.SemaphoreType.DMA((2, 2, n_copies)),
        ],
    )
    return pl.pallas_call(
        body,
        grid_spec=grid_spec,
        out_shape=jax.ShapeDtypeStruct((ROWS, MOBA_HEADS * MOBA_DH), BF16),
        input_output_aliases={7: 0},
        compiler_params=_params("arbitrary", "arbitrary"),
        name="sample_attend",
    )(page_table, blk_ids, mq, mk, mv, cache_k, cache_v, moba_o)


def _softmax_attend(q, k, v):
    s = lax.dot_general(q, k, _NT, preferred_element_type=F32)
    m = jnp.max(s, axis=-1, keepdims=True)
    p = jnp.exp(s - m)
    l = jnp.sum(p, axis=-1, keepdims=True)
    return jnp.dot(p.astype(v.dtype), v, preferred_element_type=F32) / l


def _mem_attend_prompt(cq, mem_k, mem_v):
    tq = 1024
    w = MEM_HEADS * MEM_DH
    scale = MEM_DH ** -0.5

    def body(q_ref, k_ref, v_ref, _, o_ref):
        for h in range(MEM_HEADS):
            cols = slice(h * MEM_DH, (h + 1) * MEM_DH)
            q = (q_ref[:, cols] * scale).astype(BF16)
            o_ref[:, cols] = _softmax_attend(q, k_ref[:, cols].astype(BF16),
                                             v_ref[:, cols].astype(BF16)).astype(o_ref.dtype)

    kv_spec = pl.BlockSpec((N_MEM, w), lambda r: (0, 0))
    return pl.pallas_call(
        body,
        grid=(SEQ // tq,),
        in_specs=[pl.BlockSpec((tq, w), lambda r: (r, 0)), kv_spec, kv_spec, pl.BlockSpec(memory_space=pl.ANY)],
        out_specs=pl.BlockSpec((tq, w), lambda r: (r, 0)),
        out_shape=jax.ShapeDtypeStruct((ROWS, w), BF16),
        input_output_aliases={3: 0},
        compiler_params=_params("arbitrary"),
        name="mem_attend_prompt",
    )(cq, mem_k, mem_v, _zero_slab(w))


def _mem_attend_sample(cq, mem_k, mem_v, mem_o):
    nb = SEQS_PER_STEP
    P = DEC_PAD
    w = MEM_HEADS * MEM_DH
    scale = MEM_DH ** -0.5
    row_blk0 = SEQ // (nb * P)

    def body(q_ref, k_ref, v_ref, _, o_ref):
        qa = q_ref[...] * scale
        out_rows = []
        for bb in range(nb):
            heads = []
            for h in range(MEM_HEADS):
                cols = slice(h * MEM_DH, (h + 1) * MEM_DH)
                heads.append(_softmax_attend(qa[bb * P:(bb + 1) * P, cols], k_ref[bb, :, h, :],
                                             v_ref[bb, :, h, :]))
            out_rows.append(jnp.concatenate(heads, axis=1))
        o_ref[...] = jnp.concatenate(out_rows, axis=0).astype(o_ref.dtype)

    kv_spec = pl.BlockSpec((nb, N_MEM, MEM_HEADS, MEM_DH), lambda g: (g, 0, 0, 0))
    row_spec = pl.BlockSpec((nb * P, w), lambda g: (row_blk0 + g, 0))
    return pl.pallas_call(
        body,
        grid=(DEC_BATCH // nb,),
        in_specs=[row_spec, kv_spec, kv_spec, pl.BlockSpec(memory_space=pl.ANY)],
        out_specs=row_spec,
        out_shape=jax.ShapeDtypeStruct((ROWS, w), BF16),
        input_output_aliases={3: 0},
        compiler_params=_params("arbitrary"),
        name="mem_attend_sample",
    )(cq, mem_k, mem_v, mem_o)


def kernel(x_prompt, x_sample, cache_k, cache_v, cache_mem_k, cache_mem_v, state_ret, page_table, mem_prompt, attn_norm_w, w_in, moba_q_norm, moba_k_norm, mem_q_norm, mem_k_norm, mem_norm_w, w_mem_kv, ret_gn_w, w_ret_o, w_moba_o, w_mem_o, w_out, mlp_norm_w, w_up, w_down):
    hd_mem = MEM_HEADS * MEM_DH

    xs = jnp.pad(x_sample, ((0, 0), (0, DEC_PAD - DEC_SEQ), (0, 0))).reshape(ROWS_S, D_MODEL)
    x, xn = _stack_rmsnorm(x_prompt[0], xs, attn_norm_w)

    half = RET_DK // 2
    inv = ROPE_BASE ** (-jnp.arange(half, dtype=F32) * (2.0 / RET_DK))
    inv = jnp.concatenate([inv, inv]).reshape(1, LANES)
    log_g = jnp.log1p(-jnp.exp2(-5.0 - jnp.arange(RET_HEADS, dtype=F32)))
    cos_t, sin_t = _rope_tables(inv)

    rope_args = [(cos_t, (TM, LANES), lambda i, j: (i, 0)), (sin_t, (TM, LANES), lambda i, j: (i, 0))]

    def norm_arg(wv):
        return [(wv.reshape(1, -1), (1, wv.shape[0]), lambda i, j: (0, 0))]

    def proj(col0, n_cols, epi, args, dt, name):
        return _matmul(xn, w_in, col0, n_cols, TM, TN, epi, args, [dt], name)

    rq = proj(C_RQ, 1024, functools.partial(_epi_rope, 1.0), rope_args, BF16, "proj_rq")
    rk = proj(C_RK, 1024, functools.partial(_epi_rope, RET_DK ** -0.5), rope_args, BF16, "proj_rk")
    rv = proj(C_RV, 2048, _epi_plain, [], BF16, "proj_rv")
    rg = proj(C_RG, 2048, _epi_plain, [], F32, "proj_rg")
    mq = proj(C_MQ, 1024, functools.partial(_epi_headnorm, MOBA_DH), norm_arg(moba_q_norm), F32, "proj_mq")
    mk = proj(C_MK, 1024, functools.partial(_epi_headnorm, MOBA_DH), norm_arg(moba_k_norm), F32, "proj_mk")
    mv = proj(C_MV, 1024, _epi_plain, [], F32, "proj_mv")
    cq = proj(C_CQ, 1024, functools.partial(_epi_headnorm, MEM_DH), norm_arg(mem_q_norm), F32, "proj_cq")
    gates = proj(C_GATE, 3 * D_MODEL, _epi_sigmoid, [], BF16, "proj_gates")

    mn = _rmsnorm_cast(mem_prompt[0], mem_norm_w, N_MEM)
    mem_k_p = _matmul(mn, w_mem_kv, 0, hd_mem, N_MEM, TN, functools.partial(_epi_headnorm, MEM_DH),
                      norm_arg(mem_k_norm), [F32], "mem_k")
    mem_v_p = _matmul(mn, w_mem_kv, hd_mem, hd_mem, N_MEM, TN, _epi_plain, [], [F32], "mem_v")

    gn_w = ret_gn_w.reshape(1, -1)
    ret_o, ret_state_p = _retention_prompt(log_g, rq, rk, rv, rg, gn_w)
    ret_o, ret_state_s = _retention_sample(log_g, rq, rk, rv, rg, gn_w, state_ret, ret_o)

    moba_o, kmean_s = _moba_prompt(mq, mk, mv, cache_k, page_table)
    ids = _sample_block_select(mq, kmean_s)
    ids = ids[:, :, :DEC_SEQ, :MOBA_TOPK].transpose(0, 2, 1, 3).reshape(-1)
    moba_o = _sample_attend(mq, mk, mv, cache_k, cache_v, page_table, ids, moba_o)

    mem_o = _mem_attend_prompt(cq, mem_k_p, mem_v_p)
    mem_o = _mem_attend_sample(cq, cache_mem_k, cache_mem_v, mem_o)

    tn = TN_EPI
    merged = _gated_branch_merge([ret_o, moba_o, mem_o],
                                 [w_ret_o.astype(BF16), w_moba_o.astype(BF16), w_mem_o.astype(BF16)], gates)
    hres = _matmul(merged, w_out, 0, D_MODEL, TM, tn, _epi_residual,
                   [(x, (TM, tn), lambda i, j: (i, j))], [F32], "out_proj")
    hn = _rmsnorm_cast(hres, mlp_norm_w, 384)
    u2 = _matmul(hn, w_up, 0, D_FF, TM, TN, _epi_relu_sq, [], [BF16], "mlp_up")
    tm_down = TM // 2
    y = _matmul(u2, w_down.astype(BF16), 0, D_MODEL, tm_down, tn, _epi_residual,
                [(hres, (tm_down, tn), lambda i, j: (i, j))], [F32], "mlp_down")

    def sample_rows(a):
        return a.reshape(DEC_BATCH, DEC_PAD, *a.shape[1:])[:, :DEC_SEQ]

    shape4 = (MOBA_HEADS, MOBA_DH)
    return (
        y[:SEQ][None],
        sample_rows(y[SEQ:]),
        mk[:SEQ].reshape(1, SEQ, *shape4),
        mv[:SEQ].reshape(1, SEQ, *shape4),
        ret_state_p[None],
        mem_k_p.reshape(1, N_MEM, MEM_HEADS, MEM_DH),
        mem_v_p.reshape(1, N_MEM, MEM_HEADS, MEM_DH),
        sample_rows(mk[SEQ:]).reshape(DEC_BATCH, DEC_SEQ, *shape4),
        sample_rows(mv[SEQ:]).reshape(DEC_BATCH, DEC_SEQ, *shape4),
        ret_state_s,
    )
```

```python
import functools

import jax
import jax.numpy as jnp
from jax import lax
from jax.experimental import pallas as pl
from jax.experimental.pallas import tpu as pltpu

F32 = jnp.float32
BF16 = jnp.bfloat16

D_MODEL = 2048
SEQ = 8192
DEC_BATCH = 32
DEC_SEQ = 4
DEC_PAD = 8
PAST_LEN = 16384
PAGE_SIZE = 128
N_MEM = 256
RET_HEADS = 8
RET_DK = 128
RET_DV = 256
RET_CHUNK = 128
ROPE_BASE = 10000.0
MOBA_HEADS = 8
MOBA_DH = 128
MOBA_BLOCK = 256
MOBA_TOPK = 3
MEM_HEADS = 4
MEM_DH = 256
D_FF = 4 * D_MODEL
EPS = 1e-6
GN_EPS = 1e-5
NEG = -1e30
LOG2_E = 1.4426950408889634

ROWS_S = DEC_BATCH * DEC_PAD
ROWS = SEQ + ROWS_S
TM = 1056
TN = 1024
TN_EPI = 512
LANES = 128
SEQS_PER_STEP = 2
VMEM_LIMIT = 56 * 1024 * 1024

C_RQ, C_RK, C_RV, C_RG = 0, 1024, 2048, 4096
C_MQ, C_MK, C_MV, C_CQ, C_GATE = 6144, 7168, 8192, 9216, 10240

_NT = (((1,), (1,)), ((), ()))


def _iota(shape, dim):
    return lax.broadcasted_iota(jnp.int32, shape, dim)


def _params(*sem):
    return pltpu.CompilerParams(dimension_semantics=sem, vmem_limit_bytes=VMEM_LIMIT)


def _zero_slab(width):
    return jnp.zeros((ROWS, width), BF16)


def _rmsnorm_cast(x, w, tm):
    rows, d = x.shape

    def body(x_ref, w_ref, o_ref):
        xf = x_ref[...]
        y = xf * lax.rsqrt(jnp.mean(xf * xf, axis=-1, keepdims=True) + EPS)
        o_ref[...] = (y * w_ref[...]).astype(o_ref.dtype)

    return pl.pallas_call(
        body,
        grid=(rows // tm,),
        in_specs=[pl.BlockSpec((tm, d), lambda i: (i, 0)), pl.BlockSpec((1, d), lambda i: (0, 0))],
        out_specs=pl.BlockSpec((tm, d), lambda i: (i, 0)),
        out_shape=jax.ShapeDtypeStruct((rows, d), BF16),
        compiler_params=_params("arbitrary"),
        name="rmsnorm_cast",
    )(x, w.reshape(1, d))


def _stack_rmsnorm(xp, xs, w):
    tm = ROWS_S
    n_p = SEQ // tm
    d = xp.shape[1]

    def body(xp_ref, xs_ref, w_ref, x_ref, o_ref):
        xf = jnp.where(pl.program_id(0) < n_p, xp_ref[...], xs_ref[...])
        x_ref[...] = xf
        y = xf * lax.rsqrt(jnp.mean(xf * xf, axis=-1, keepdims=True) + EPS)
        o_ref[...] = (y * w_ref[...]).astype(o_ref.dtype)

    return pl.pallas_call(
        body,
        grid=(n_p + 1,),
        in_specs=[pl.BlockSpec((tm, d), lambda i: (jnp.minimum(i, n_p - 1), 0)),
                  pl.BlockSpec((tm, d), lambda i: (0, 0)),
                  pl.BlockSpec((1, d), lambda i: (0, 0))],
        out_specs=[pl.BlockSpec((tm, d), lambda i: (i, 0))] * 2,
        out_shape=[jax.ShapeDtypeStruct((ROWS, d), F32), jax.ShapeDtypeStruct((ROWS, d), BF16)],
        compiler_params=_params("arbitrary"),
        name="stack_rmsnorm",
    )(xp, xs, w.reshape(1, d))


def _rope_tables(inv):
    tm = TM

    def body(inv_ref, c_ref, s_ref):
        row = pl.program_id(0) * tm + _iota((tm, LANES), 0)
        pos = jnp.where(row < SEQ, row, PAST_LEN + ((row - SEQ) & (DEC_PAD - 1)))
        ang = pos.astype(F32) * inv_ref[...]
        sn = jnp.sin(ang)
        c_ref[...] = jnp.cos(ang)
        s_ref[...] = jnp.where(_iota((tm, LANES), 1) < RET_DK // 2, -sn, sn)

    return pl.pallas_call(
        body,
        grid=(ROWS // tm,),
        in_specs=[pl.BlockSpec((1, LANES), lambda i: (0, 0))],
        out_specs=[pl.BlockSpec((tm, LANES), lambda i: (i, 0))] * 2,
        out_shape=[jax.ShapeDtypeStruct((ROWS, LANES), F32)] * 2,
        compiler_params=_params("arbitrary"),
        name="rope_tables",
    )(inv)


def _matmul(a, w, col0, n_cols, tm, tn, epi_fn, epi_args, out_dtypes, name, row0=0, n_rows=None):
    k = a.shape[1]
    m = a.shape[0] if n_rows is None else n_rows
    assert m % tm == 0 and row0 % tm == 0 and n_cols % tn == 0 and col0 % tn == 0
    jb0, ib0 = col0 // tn, row0 // tm
    n_epi, n_out = len(epi_args), len(out_dtypes)
    cast_once = w.dtype != BF16

    def body(a_ref, w_ref, *rest):
        epi_refs, out_refs = rest[:n_epi], rest[n_epi:n_epi + n_out]
        if cast_once:
            wb_ref = rest[-1]

            @pl.when(pl.program_id(1) == 0)
            def _():
                wb_ref[...] = w_ref[...].astype(BF16)
        else:
            wb_ref = w_ref
        acc = jnp.dot(a_ref[...], wb_ref[...], preferred_element_type=F32)
        for o_ref, o in zip(out_refs, epi_fn(acc, *epi_refs)):
            o_ref[...] = o.astype(o_ref.dtype)

    def spec(block, index_map):
        if cast_once:
            return pl.BlockSpec(block, lambda j, i: index_map(i, j))
        return pl.BlockSpec(block, index_map)

    in_specs = [spec((tm, k), lambda i, j: (ib0 + i, 0)), spec((k, tn), lambda i, j: (0, jb0 + j))]
    in_specs += [spec(bs, lambda i, j, im=im: im(ib0 + i, j)) for _, bs, im in epi_args]
    n_i, n_j = m // tm, n_cols // tn
    outs = pl.pallas_call(
        body,
        grid=(n_j, n_i) if cast_once else (n_i, n_j),
        in_specs=in_specs,
        out_specs=[spec((tm, tn), lambda i, j: (i, j)) for _ in out_dtypes],
        out_shape=[jax.ShapeDtypeStruct((m, n_cols), dt) for dt in out_dtypes],
        scratch_shapes=[pltpu.VMEM((k, tn), BF16)] if cast_once else [],
        compiler_params=_params("arbitrary", "arbitrary"),
        name=name,
    )(a, w, *[x for x, _, _ in epi_args])
    return outs[0] if n_out == 1 else outs


def _epi_plain(acc):
    return (acc,)


def _epi_sigmoid(acc):
    return (1.0 / (1.0 + jnp.exp(-acc)),)


def _epi_relu_sq(acc):
    u = jnp.maximum(acc, 0.0)
    return (u * u,)


def _epi_rope(scale, acc, c_ref, s_ref):
    c, s = c_ref[...], s_ref[...]
    pieces = []
    for hh in range(acc.shape[1] // RET_DK):
        x = acc[:, hh * RET_DK:(hh + 1) * RET_DK]
        y = x * c + pltpu.roll(x, RET_DK // 2, 1) * s
        pieces.append(y * scale if scale != 1.0 else y)
    return (jnp.concatenate(pieces, axis=1),)


def _epi_headnorm(hw, acc, w_ref):
    w = w_ref[...]
    pieces = []
    for hh in range(acc.shape[1] // hw):
        x = acc[:, hh * hw:(hh + 1) * hw]
        pieces.append(x * lax.rsqrt(jnp.mean(x * x, axis=-1, keepdims=True) + EPS) * w)
    return (jnp.concatenate(pieces, axis=1),)


def _epi_residual(acc, x_ref):
    return (x_ref[...] + acc,)


def _gated_branch_merge(branches, weights, gates):
    n_b = len(branches)
    tm, tn = TM, TN_EPI

    def body(*refs):
        a_refs, w_refs, g_refs, o_ref = refs[:n_b], refs[n_b:2 * n_b], refs[2 * n_b:3 * n_b], refs[3 * n_b]
        acc = None
        for a_ref, w_ref, g_ref in zip(a_refs, w_refs, g_refs):
            term = g_ref[...].astype(F32) * jnp.dot(a_ref[...], w_ref[...], preferred_element_type=F32)
            acc = term if acc is None else acc + term
        o_ref[...] = acc.astype(o_ref.dtype)

    in_specs = [pl.BlockSpec((tm, a.shape[1]), lambda i, j: (i, 0)) for a in branches]
    in_specs += [pl.BlockSpec((w.shape[0], tn), lambda i, j: (0, j)) for w in weights]
    in_specs += [pl.BlockSpec((tm, tn), lambda i, j, b=b: (i, b * (D_MODEL // tn) + j)) for b in range(n_b)]
    return pl.pallas_call(
        body,
        grid=(ROWS // tm, D_MODEL // tn),
        in_specs=in_specs,
        out_specs=pl.BlockSpec((tm, tn), lambda i, j: (i, j)),
        out_shape=jax.ShapeDtypeStruct((ROWS, D_MODEL), BF16),
        compiler_params=_params("arbitrary", "arbitrary"),
        name="branch_merge",
    )(*branches, *weights, *([gates] * n_b))


def _gated_groupnorm(o, rg, gw):
    mu = jnp.mean(o, axis=-1, keepdims=True)
    d = o - mu
    var = jnp.mean(d * d, axis=-1, keepdims=True)
    rn = d * lax.rsqrt(var + GN_EPS) * gw
    return rg / (1.0 + jnp.exp(-rg)) * rn


def _retention_prompt(log_g, rq, rk, rv, rg, gn_w):
    C = RET_CHUNK
    nc = SEQ // C
    HP = 4

    def body(lg_ref, q_ref, k_ref, v_ref, rg_ref, gw_ref, _, o_ref, st_ref, state, dmask_t, qdec_t, kdec_t):
        hp, c = pl.program_id(0), pl.program_id(1)

        @pl.when(c == 0)
        def _():
            state[...] = jnp.zeros_like(state)
            diff = (_iota((C, C), 0) - _iota((C, C), 1)).astype(F32)
            ii = _iota((C, 1), 0).astype(F32)
            for hh in range(HP):
                lg = lg_ref[hp * HP + hh]
                dmask_t[hh] = jnp.where(diff >= 0, jnp.exp(jnp.maximum(diff, 0.0) * lg), 0.0)
                qdec_t[hh] = jnp.broadcast_to(jnp.exp((ii + 1.0) * lg), (C, RET_DV))
                kdec_t[hh] = jnp.broadcast_to(jnp.exp((C - 1.0 - ii) * lg), (C, RET_DK))

        for hh in range(HP):
            lg = lg_ref[hp * HP + hh]
            q = q_ref[:, hh * RET_DK:(hh + 1) * RET_DK]
            k = k_ref[:, hh * RET_DK:(hh + 1) * RET_DK]
            v = v_ref[:, hh * RET_DV:(hh + 1) * RET_DV]
            s = lax.dot_general(q, k, _NT, preferred_element_type=F32) * dmask_t[hh]
            c_decay = jnp.exp(jnp.zeros((1, RET_DV), F32) + C * lg)
            s_old = state[hh]
            o = (jnp.dot(s.astype(BF16), v, preferred_element_type=F32)
                 + jnp.dot(q, s_old.astype(BF16), preferred_element_type=F32) * qdec_t[hh])
            kd_t = (k.astype(F32) * kdec_t[hh]).T.astype(BF16)
            s_new = s_old * c_decay + jnp.dot(kd_t, v, preferred_element_type=F32)
            state[hh] = s_new
            cols = slice(hh * RET_DV, (hh + 1) * RET_DV)
            o_ref[:, cols] = _gated_groupnorm(o, rg_ref[:, cols], gw_ref[:, cols]).astype(o_ref.dtype)

        @pl.when(c == nc - 1)
        def _():
            st_ref[...] = state[...]

    return pl.pallas_call(
        body,
        grid=(RET_HEADS // HP, nc),
        in_specs=[
            pl.BlockSpec(memory_space=pltpu.SMEM),
            pl.BlockSpec((C, HP * RET_DK), lambda h, c: (c, h)),
            pl.BlockSpec((C, HP * RET_DK), lambda h, c: (c, h)),
            pl.BlockSpec((C, HP * RET_DV), lambda h, c: (c, h)),
            pl.BlockSpec((C, HP * RET_DV), lambda h, c: (c, h)),
            pl.BlockSpec((1, HP * RET_DV), lambda h, c: (0, h)),
            pl.BlockSpec(memory_space=pl.ANY),
        ],
        out_specs=[
            pl.BlockSpec((C, HP * RET_DV), lambda h, c: (c, h)),
            pl.BlockSpec((HP, RET_DK, RET_DV), lambda h, c: (h, 0, 0)),
        ],
        out_shape=[
            jax.ShapeDtypeStruct((ROWS, RET_HEADS * RET_DV), BF16),
            jax.ShapeDtypeStruct((RET_HEADS, RET_DK, RET_DV), F32),
        ],
        input_output_aliases={6: 0},
        scratch_shapes=[
            pltpu.VMEM((HP, RET_DK, RET_DV), F32),
            pltpu.VMEM((HP, C, C), F32),
            pltpu.VMEM((HP, C, RET_DV), F32),
            pltpu.VMEM((HP, C, RET_DK), F32),
        ],
        compiler_params=_params("arbitrary", "arbitrary"),
        name="retention_prompt",
    )(log_g, rq, rk, rv, rg, gn_w, _zero_slab(RET_HEADS * RET_DV))


def _retention_sample(log_g, rq, rk, rv, rg, gn_w, state_ret, ret_o):
    T = DEC_SEQ
    P = DEC_PAD
    nb = SEQS_PER_STEP
    row_blk0 = SEQ // (nb * P)

    def body(lg_ref, q_ref, k_ref, v_ref, rg_ref, gw_ref, st_in_ref, _, o_ref, st_ref):
        qa, ka, va = q_ref[...].astype(F32), k_ref[...].astype(F32), v_ref[...].astype(F32)
        rga, gw = rg_ref[...], gw_ref[...]
        ri, ci = _iota((P, P), 0), _iota((P, P), 1)
        diff = (ri - ci).astype(F32)
        ii = _iota((P, 1), 0)
        iif = ii.astype(F32)
        live = ii < T
        out_rows = []
        for bb in range(nb):
            rows = slice(bb * P, (bb + 1) * P)
            out_heads = []
            for h in range(RET_HEADS):
                lg = lg_ref[h]
                q = qa[rows, h * RET_DK:(h + 1) * RET_DK]
                k = jnp.where(live, ka[rows, h * RET_DK:(h + 1) * RET_DK], 0.0)
                v = jnp.where(live, va[rows, h * RET_DV:(h + 1) * RET_DV], 0.0)
                dmask = jnp.where(diff >= 0, jnp.exp(jnp.maximum(diff, 0.0) * lg), 0.0)
                s = lax.dot_general(q, k, _NT, preferred_element_type=F32) * dmask
                q_decay = jnp.exp((iif + 1.0) * lg)
                k_decay = jnp.exp((T - 1.0 - iif) * lg)
                c_decay = jnp.exp(jnp.zeros((1, RET_DV), F32) + T * lg)
                s_old = st_in_ref[bb, h]
                o = (jnp.dot(s, v, preferred_element_type=F32)
                     + jnp.dot(q, s_old, preferred_element_type=F32) * q_decay)
                upd = lax.dot_general(k * k_decay, v, (((0,), (0,)), ((), ())),
                                      preferred_element_type=F32)
                st_ref[bb, h] = s_old * c_decay + upd
                out_heads.append(_gated_groupnorm(
                    o, rga[rows, h * RET_DV:(h + 1) * RET_DV], gw[:, h * RET_DV:(h + 1) * RET_DV]))
            out_rows.append(jnp.concatenate(out_heads, axis=1))
        o_ref[...] = jnp.concatenate(out_rows, axis=0).astype(o_ref.dtype)

    wq, wv = RET_HEADS * RET_DK, RET_HEADS * RET_DV
    return pl.pallas_call(
        body,
        grid=(DEC_BATCH // nb,),
        in_specs=[
            pl.BlockSpec(memory_space=pltpu.SMEM),
            pl.BlockSpec((nb * P, wq), lambda g: (row_blk0 + g, 0)),
            pl.BlockSpec((nb * P, wq), lambda g: (row_blk0 + g, 0)),
            pl.BlockSpec((nb * P, wv), lambda g: (row_blk0 + g, 0)),
            pl.BlockSpec((nb * P, wv), lambda g: (row_blk0 + g, 0)),
            pl.BlockSpec((1, wv), lambda g: (0, 0)),
            pl.BlockSpec((nb, RET_HEADS, RET_DK, RET_DV), lambda g: (g, 0, 0, 0)),
            pl.BlockSpec(memory_space=pl.ANY),
        ],
        out_specs=[
            pl.BlockSpec((nb * P, wv), lambda g: (row_blk0 + g, 0)),
            pl.BlockSpec((nb, RET_HEADS, RET_DK, RET_DV), lambda g: (g, 0, 0, 0)),
        ],
        out_shape=[
            jax.ShapeDtypeStruct((ROWS, wv), BF16),
            jax.ShapeDtypeStruct((DEC_BATCH, RET_HEADS, RET_DK, RET_DV), F32),
        ],
        input_output_aliases={7: 0},
        compiler_params=_params("arbitrary"),
        name="retention_sample",
    )(log_g, rq, rk, rv, rg, gn_w, state_ret, ret_o)


def _moba_prompt(mq, mk, mv, cache_k, page_table):
    BL = MOBA_BLOCK
    nblk = SEQ // BL
    TQ = 2 * BL
    scale = MOBA_DH ** -0.5

    n_sblk = PAST_LEN // MOBA_BLOCK
    ppb = MOBA_BLOCK // PAGE_SIZE
    UNITS = DEC_BATCH * n_sblk
    n_tiles = SEQ // TQ
    trips_per_head = (n_tiles // 2) ** 2
    U_TILE, U_TRIP = 4, 3
    assert MOBA_HEADS * (n_tiles * U_TILE + trips_per_head * U_TRIP) == UNITS
    AHEAD = 12
    RING = AHEAD + max(U_TILE, U_TRIP)
    assert RING & (RING - 1) == 0 and UNITS & (UNITS - 1) == 0 and n_sblk & (n_sblk - 1) == 0
    sblk_shift = n_sblk.bit_length() - 1

    def body(pt_ref, q_ref, k_ref, v_ref, _, ck_ref, o_ref, km_ref, kaug, vt, kmean, s_a, s_b, p_a, p_b,
             ring, sem):
        h, ti = pl.program_id(0), pl.program_id(1)

        def unit_copies(u):
            uu = u & (UNITS - 1)
            b, j = uu >> sblk_shift, uu & (n_sblk - 1)
            slot = u & (RING - 1)
            return [pltpu.make_async_copy(ck_ref.at[pt_ref[b, j * ppb + pg]], ring.at[slot, pg],
                                          sem.at[slot, pg]) for pg in range(ppb)]

        def fetch(u0, n):
            for i in range(n):
                for cp in unit_copies(u0 + AHEAD + i):
                    cp.start()

        def reduce(u0, n):
            for i in range(n):
                u = u0 + i
                for cp in unit_copies(u):
                    cp.wait()
                slot = u & (RING - 1)
                lanes8 = (8, PAGE_SIZE // 8, MOBA_HEADS, MOBA_DH)
                parts = [jnp.sum(ring[slot, pg].reshape(lanes8), axis=1) for pg in range(ppb)]
                tot = jnp.sum(functools.reduce(lambda a, b: a + b, parts), axis=0)
                km_ref[u >> sblk_shift, u & (n_sblk - 1)] = tot * (1.0 / MOBA_BLOCK)

        @pl.when(jnp.logical_and(h == 0, ti == 0))
        def _():
            fetch(-AHEAD, AHEAD)

        u_tile = (h * n_tiles + ti) * U_TILE + (h * trips_per_head + ((ti * ti) >> 2)) * U_TRIP
        fetch(u_tile, U_TILE)
        reduce(u_tile, U_TILE)

        @pl.when(ti == 0)
        def _():
            kmean[...] = jnp.zeros_like(kmean)
            lane = _iota((BL, LANES), 1)

            def prep(jb, carry):
                r0 = pl.multiple_of(jb * BL, BL)
                kb = k_ref[pl.ds(r0, BL), :]
                kmean[pl.ds(jb, 1), :] = jnp.mean(kb, axis=0, keepdims=True)
                kaug[pl.ds(r0, BL), 0:MOBA_DH] = kb.astype(BF16)
                kaug[pl.ds(r0, BL), MOBA_DH:2 * MOBA_DH] = (lane == jb).astype(BF16)
                vt[:, pl.ds(r0, BL)] = v_ref[pl.ds(r0, BL), :].T.astype(BF16)
                return carry

            lax.fori_loop(0, nblk, prep, 0)

        q = q_ref[...]
        gt = lax.dot_general(kmean[0:nblk, :], q, _NT, precision=lax.Precision.HIGHEST,
                             preferred_element_type=F32)
        n_sg = nblk // 8
        row = _iota((nblk, TQ), 0)
        row8 = _iota((8, TQ), 0)
        second = jnp.where(_iota((8, TQ), 1) >= BL, 1, 0)
        own = 2 * ti + second
        gtv = jnp.where(row < 2 * ti + jnp.where(_iota((nblk, TQ), 1) >= BL, 1, 0), gt, -jnp.inf)
        cnts = [jnp.zeros((8, TQ), jnp.int32) for _ in range(n_sg)]
        for blk in range(nblk):
            g_blk = gtv[blk:blk + 1, :]
            for sg in range(n_sg):
                g_sg = gt[8 * sg:8 * sg + 8, :]
                if 8 * sg > blk:
                    cnts[sg] = cnts[sg] + jnp.where(g_blk >= g_sg, 1, 0)
                elif 8 * sg + 7 < blk:
                    cnts[sg] = cnts[sg] + jnp.where(g_blk > g_sg, 1, 0)
                else:
                    cnts[sg] = cnts[sg] + jnp.where(row8 + 8 * sg > blk, jnp.where(g_blk >= g_sg, 1, 0),
                                                    jnp.where(g_blk > g_sg, 1, 0))
        picked = [(cnts[sg] < MOBA_TOPK) & (row8 + 8 * sg < own) for sg in range(n_sg)]
        qs = (q * (scale * LOG2_E)).astype(BF16)

        parts = [jnp.where(kp, 0.0, NEG) for kp in picked] + [jnp.zeros((LANES - nblk, TQ), F32)]
        bias = jnp.concatenate(parts, axis=0).T
        blk_lane = _iota((TQ, LANES), 1)
        own_row = 2 * ti + jnp.where(_iota((TQ, LANES), 0) >= BL, 1, 0)

        def with_bias(b):
            return jnp.concatenate([qs, b.astype(BF16)], axis=1)

        qa_past = with_bias(jnp.where(blk_lane < 2 * ti, bias, NEG))
        qa_own = with_bias(jnp.where(blk_lane == own_row, 0.0, bias))

        def scores(qx, g):
            c0 = pl.multiple_of(g * TQ, TQ)
            return lax.dot_general(kaug[pl.ds(c0, TQ), :], qx, _NT, preferred_element_type=F32)

        def values(g):
            c0 = pl.multiple_of(g * TQ, TQ)
            return vt[:, pl.ds(c0, TQ)]

        CH = 64

        def fold8(x, fn):
            return fn(x.reshape(CH // 8, 8, TQ), axis=0)

        def softmax_stage(s_ref, p_ref, m, l):
            mx8 = jnp.broadcast_to(m, (8, TQ))
            for c in range(TQ // CH):
                mx8 = jnp.maximum(mx8, fold8(s_ref[c * CH:(c + 1) * CH, :], jnp.max))
            m_new = jnp.max(mx8, axis=0, keepdims=True)
            alpha = jnp.exp2(m - m_new)
            sum8 = jnp.zeros((8, TQ), F32)
            for c in range(TQ // CH):
                pc = jnp.exp2(s_ref[c * CH:(c + 1) * CH, :] - m_new)
                sum8 = sum8 + fold8(pc, jnp.sum)
                p_ref[c * CH:(c + 1) * CH, :] = pc.astype(BF16)
            return m_new, l * alpha + jnp.sum(sum8, axis=0, keepdims=True), alpha

        last = SEQ // TQ - 1

        s_a[...] = jnp.where(_iota((TQ, TQ), 0) <= _iota((TQ, TQ), 1), scores(qa_own, ti), NEG)
        m, l, _ = softmax_stage(s_a, p_a, jnp.full((1, TQ), NEG, F32), jnp.zeros((1, TQ), F32))
        s_b[...] = scores(qa_past, 0)
        acc = jnp.zeros((MOBA_DH, TQ), F32)

        def trip(k, carry):
            m, l, acc = carry
            g0 = 2 * k
            u_trip = u_tile + U_TILE + k * U_TRIP
            fetch(u_trip, U_TRIP)
            s_a[...] = scores(qa_past, jnp.minimum(g0 + 1, last))
            pv = jnp.dot(values(jnp.where(k == 0, ti, g0 - 1)), p_a[...], preferred_element_type=F32)
            m, l, alpha = softmax_stage(s_b, p_b, m, l)
            acc = (acc + pv) * alpha
            s_b[...] = scores(qa_past, jnp.minimum(g0 + 2, last))
            pv = jnp.dot(values(g0), p_b[...], preferred_element_type=F32)
            m, l, alpha = softmax_stage(s_a, p_a, m, l)
            reduce(u_trip, U_TRIP)
            return m, l, (acc + pv) * alpha

        trips = (ti + 1) >> 1
        m, l, acc = lax.fori_loop(0, trips, trip, (m, l, acc))
        acc = acc + jnp.dot(values(jnp.where(trips == 0, ti, 2 * trips - 1)), p_a[...],
                            preferred_element_type=F32)
        o_ref[...] = (acc / l).T.astype(o_ref.dtype)

        @pl.when(jnp.logical_and(h == MOBA_HEADS - 1, ti == n_tiles - 1))
        def _():
            for u in range(UNITS, UNITS + AHEAD):
                for cp in unit_copies(u):
                    cp.wait()

    grid_spec = pltpu.PrefetchScalarGridSpec(
        num_scalar_prefetch=1,
        grid=(MOBA_HEADS, n_tiles),
        in_specs=[
            pl.BlockSpec((TQ, MOBA_DH), lambda h, i, pt: (i, h)),
            pl.BlockSpec((SEQ, MOBA_DH), lambda h, i, pt: (0, h)),
            pl.BlockSpec((SEQ, MOBA_DH), lambda h, i, pt: (0, h)),
            pl.BlockSpec(memory_space=pl.ANY),
            pl.BlockSpec(memory_space=pl.ANY),
        ],
        out_specs=[
            pl.BlockSpec((TQ, MOBA_DH), lambda h, i, pt: (i, h)),
            pl.BlockSpec((DEC_BATCH, n_sblk, MOBA_HEADS, MOBA_DH), lambda h, i, pt: (0, 0, 0, 0)),
        ],
        scratch_shapes=[
            pltpu.VMEM((SEQ, 2 * MOBA_DH), BF16),
            pltpu.VMEM((MOBA_DH, SEQ), BF16),
            pltpu.VMEM((LANES, MOBA_DH), F32),
            pltpu.VMEM((TQ, TQ), F32),
            pltpu.VMEM((TQ, TQ), F32),
            pltpu.VMEM((TQ, TQ), BF16),
            pltpu.VMEM((TQ, TQ), BF16),
            pltpu.VMEM((RING, ppb, PAGE_SIZE, MOBA_HEADS, MOBA_DH), F32),
            pltpu.SemaphoreType.DMA((RING, ppb)),
        ],
    )
    return pl.pallas_call(
        body,
        grid_spec=grid_spec,
        out_shape=[
            jax.ShapeDtypeStruct((ROWS, MOBA_HEADS * MOBA_DH), BF16),
            jax.ShapeDtypeStruct((DEC_BATCH, n_sblk, MOBA_HEADS, MOBA_DH), F32),
        ],
        input_output_aliases={4: 0},
        compiler_params=_params("arbitrary", "arbitrary"),
        name="moba_prompt",
    )(page_table, mq, mk, mv, _zero_slab(MOBA_HEADS * MOBA_DH), cache_k)


def _sample_block_select(mq, kmean):
    nblk = PAST_LEN // MOBA_BLOCK
    P = DEC_PAD
    row_blk0 = SEQ // P

    def body(q_ref, km_ref, o_ref):
        q = q_ref[...]
        lane = _iota((P, LANES), 1)
        for h in range(MOBA_HEADS):
            cols = slice(h * MOBA_DH, (h + 1) * MOBA_DH)
            kmh = jnp.concatenate([km_ref[0, :, h, :], jnp.zeros((LANES - nblk, MOBA_DH), F32)], axis=0)
            gate = lax.dot_general(q[:, cols], kmh, _NT, precision=lax.Precision.HIGHEST,
                                   preferred_element_type=F32)
            g = jnp.where(lane < nblk, gate, -jnp.inf)
            ids = []
            for _ in range(MOBA_TOPK):
                mx = jnp.max(g, axis=-1, keepdims=True)
                first = jnp.min(jnp.where(g == mx, lane, 1 << 20), axis=-1, keepdims=True)
                ids.append(first)
                g = jnp.where(lane == first, -jnp.inf, g)
            o_ref[0, h] = jnp.where(lane == 0, ids[0], jnp.where(lane == 1, ids[1], ids[2]))

    return pl.pallas_call(
        body,
        grid=(DEC_BATCH,),
        in_specs=[
            pl.BlockSpec((P, MOBA_HEADS * MOBA_DH), lambda b: (row_blk0 + b, 0)),
            pl.BlockSpec((1, nblk, MOBA_HEADS, MOBA_DH), lambda b: (b, 0, 0, 0)),
        ],
        out_specs=pl.BlockSpec((1, MOBA_HEADS, P, LANES), lambda b: (b, 0, 0, 0)),
        out_shape=jax.ShapeDtypeStruct((DEC_BATCH, MOBA_HEADS, P, LANES), jnp.int32),
        compiler_params=_params("arbitrary"),
        name="sample_block_select",
    )(mq, kmean)


def _sample_attend(mq, mk, mv, cache_k, cache_v, page_table, blk_ids, moba_o):
    ppb = MOBA_BLOCK // PAGE_SIZE
    n_sel = MOBA_TOPK * MOBA_BLOCK
    nb = SEQS_PER_STEP
    n_q = nb * DEC_SEQ
    n_copies = n_q * MOBA_TOPK * ppb
    n_steps = (DEC_BATCH // nb) * MOBA_HEADS
    scale = MOBA_DH ** -0.5
    P = DEC_PAD
    row_blk0 = SEQ // (nb * P)

    def body(pt_ref, ids_ref, q_ref, kn_ref, vn_ref, ck_ref, cv_ref, _, o_ref, kbuf, vbuf, sem):
        step = pl.program_id(0) * MOBA_HEADS + pl.program_id(1)

        def copies(st, slot):
            g, h = st // MOBA_HEADS, st % MOBA_HEADS
            out = []
            for qi in range(n_q):
                b, t = g * nb + qi // DEC_SEQ, qi % DEC_SEQ
                for sl in range(MOBA_TOPK):
                    blk = ids_ref[((b * DEC_SEQ + t) * MOBA_HEADS + h) * MOBA_TOPK + sl]
                    for pg in range(ppb):
                        phys = pt_ref[b, blk * ppb + pg]
                        n = (qi * MOBA_TOPK + sl) * ppb + pg
                        rows = pl.ds((sl * ppb + pg) * PAGE_SIZE, PAGE_SIZE)
                        for src, dst, which in ((ck_ref, kbuf, 0), (cv_ref, vbuf, 1)):
                            out.append(pltpu.make_async_copy(src.at[phys, :, h, :], dst.at[slot, qi, rows, :],
                                                             sem.at[slot, which, n]))
            return out

        slot = step % 2

        @pl.when(step == 0)
        def _():
            for n, cp in enumerate(copies(step, 0)):
                cp.start(priority=n % 2)

        @pl.when(step + 1 < n_steps)
        def _():
            for n, cp in enumerate(copies(step + 1, 1 - slot)):
                cp.start(priority=n % 2)

        for cp in copies(step, slot):
            cp.wait()

        q, kn, vn = q_ref[...], kn_ref[...], vn_ref[...]
        rown = _iota((P, 1), 0)
        out_rows = []
        for bb in range(nb):
            knb, vnb = kn[bb * P:(bb + 1) * P], vn[bb * P:(bb + 1) * P]
            res = jnp.zeros((P, MOBA_DH), F32)
            for t in range(DEC_SEQ):
                qi = bb * DEC_SEQ + t
                qt = q[bb * P + t:bb * P + t + 1, :] * scale
                s = jnp.sum(kbuf[slot, qi] * qt, axis=-1, keepdims=True)
                sn = jnp.where(rown <= t, jnp.sum(knb * qt, axis=-1, keepdims=True), NEG)
                m = jnp.maximum(jnp.max(s, axis=0, keepdims=True), jnp.max(sn, axis=0, keepdims=True))
                p, pn = jnp.exp(s - m), jnp.exp(sn - m)
                l = jnp.sum(p, axis=0, keepdims=True) + jnp.sum(pn, axis=0, keepdims=True)
                o = (jnp.sum(p * vbuf[slot, qi], axis=0, keepdims=True)
                     + jnp.sum(pn * vnb, axis=0, keepdims=True))
                res = jnp.where(rown == t, o / l, res)
            out_rows.append(res)
        o_ref[...] = jnp.concatenate(out_rows, axis=0).astype(o_ref.dtype)

    head_blk = pl.BlockSpec((nb * P, MOBA_DH), lambda g, h, pt, ids: (row_blk0 + g, h))
    new_blk = pl.BlockSpec((nb * P, MOBA_DH), lambda g, h, pt, ids: (g, h))
    any_spec = pl.BlockSpec(memory_space=pl.ANY)
    grid_spec = pltpu.PrefetchScalarGridSpec(
        num_scalar_prefetch=2,
        grid=(DEC_BATCH // nb, MOBA_HEADS),
        in_specs=[head_blk, new_blk, new_blk, any_spec, any_spec, any_spec],
        out_specs=head_blk,
        scratch_shapes=[
            pltpu.VMEM((2, n_q, n_sel, MOBA_DH), F32),
            pltpu.VMEM((2, n_q, n_sel, MOBA_DH), F32),
            pltpu.SemaphoreType.DMA((2, 2, n_copies)),
        ],
    )
    return pl.pallas_call(
        body,
        grid_spec=grid_spec,
        out_shape=jax.ShapeDtypeStruct((ROWS, MOBA_HEADS * MOBA_DH), BF16),
        input_output_aliases={7: 0},
        compiler_params=_params("arbitrary", "arbitrary"),
        name="sample_attend",
    )(page_table, blk_ids, mq, mk, mv, cache_k, cache_v, moba_o)


def _softmax_attend(q, k, v):
    s = lax.dot_general(q, k, _NT, preferred_element_type=F32)
    m = jnp.max(s, axis=-1, keepdims=True)
    p = jnp.exp(s - m)
    l = jnp.sum(p, axis=-1, keepdims=True)
    return jnp.dot(p.astype(v.dtype), v, preferred_element_type=F32) / l


def _mem_attend_prompt(cq, mem_k, mem_v):
    tq = 1024
    w = MEM_HEADS * MEM_DH
    scale = MEM_DH ** -0.5

    def body(q_ref, k_ref, v_ref, _, o_ref):
        for h in range(MEM_HEADS):
            cols = slice(h * MEM_DH, (h + 1) * MEM_DH)
            q = (q_ref[:, cols] * scale).astype(BF16)
            o_ref[:, cols] = _softmax_attend(q, k_ref[:, cols].astype(BF16),
                                             v_ref[:, cols].astype(BF16)).astype(o_ref.dtype)

    kv_spec = pl.BlockSpec((N_MEM, w), lambda r: (0, 0))
    return pl.pallas_call(
        body,
        grid=(SEQ // tq,),
        in_specs=[pl.BlockSpec((tq, w), lambda r: (r, 0)), kv_spec, kv_spec, pl.BlockSpec(memory_space=pl.ANY)],
        out_specs=pl.BlockSpec((tq, w), lambda r: (r, 0)),
        out_shape=jax.ShapeDtypeStruct((ROWS, w), BF16),
        input_output_aliases={3: 0},
        compiler_params=_params("arbitrary"),
        name="mem_attend_prompt",
    )(cq, mem_k, mem_v, _zero_slab(w))


def _mem_attend_sample(cq, mem_k, mem_v, mem_o):
    nb = SEQS_PER_STEP
    P = DEC_PAD
    w = MEM_HEADS * MEM_DH
    scale = MEM_DH ** -0.5
    row_blk0 = SEQ // (nb * P)

    def body(q_ref, k_ref, v_ref, _, o_ref):
        qa = q_ref[...] * scale
        out_rows = []
        for bb in range(nb):
            heads = []
            for h in range(MEM_HEADS):
                cols = slice(h * MEM_DH, (h + 1) * MEM_DH)
                heads.append(_softmax_attend(qa[bb * P:(bb + 1) * P, cols], k_ref[bb, :, h, :],
                                             v_ref[bb, :, h, :]))
            out_rows.append(jnp.concatenate(heads, axis=1))
        o_ref[...] = jnp.concatenate(out_rows, axis=0).astype(o_ref.dtype)

    kv_spec = pl.BlockSpec((nb, N_MEM, MEM_HEADS, MEM_DH), lambda g: (g, 0, 0, 0))
    row_spec = pl.BlockSpec((nb * P, w), lambda g: (row_blk0 + g, 0))
    return pl.pallas_call(
        body,
        grid=(DEC_BATCH // nb,),
        in_specs=[row_spec, kv_spec, kv_spec, pl.BlockSpec(memory_space=pl.ANY)],
        out_specs=row_spec,
        out_shape=jax.ShapeDtypeStruct((ROWS, w), BF16),
        input_output_aliases={3: 0},
        compiler_params=_params("arbitrary"),
        name="mem_attend_sample",
    )(cq, mem_k, mem_v, mem_o)


def kernel(x_prompt, x_sample, cache_k, cache_v, cache_mem_k, cache_mem_v, state_ret, page_table, mem_prompt, attn_norm_w, w_in, moba_q_norm, moba_k_norm, mem_q_norm, mem_k_norm, mem_norm_w, w_mem_kv, ret_gn_w, w_ret_o, w_moba_o, w_mem_o, w_out, mlp_norm_w, w_up, w_down):
    hd_mem = MEM_HEADS * MEM_DH

    xs = jnp.pad(x_sample, ((0, 0), (0, DEC_PAD - DEC_SEQ), (0, 0))).reshape(ROWS_S, D_MODEL)
    x, xn = _stack_rmsnorm(x_prompt[0], xs, attn_norm_w)

    half = RET_DK // 2
    inv = ROPE_BASE ** (-jnp.arange(half, dtype=F32) * (2.0 / RET_DK))
    inv = jnp.concatenate([inv, inv]).reshape(1, LANES)
    log_g = jnp.log1p(-jnp.exp2(-5.0 - jnp.arange(RET_HEADS, dtype=F32)))
    cos_t, sin_t = _rope_tables(inv)

    rope_args = [(cos_t, (TM, LANES), lambda i, j: (i, 0)), (sin_t, (TM, LANES), lambda i, j: (i, 0))]

    def norm_arg(wv):
        return [(wv.reshape(1, -1), (1, wv.shape[0]), lambda i, j: (0, 0))]

    def proj(col0, n_cols, epi, args, dt, name):
        return _matmul(xn, w_in, col0, n_cols, TM, TN, epi, args, [dt], name)

    rq = proj(C_RQ, 1024, functools.partial(_epi_rope, 1.0), rope_args, BF16, "proj_rq")
    rk = proj(C_RK, 1024, functools.partial(_epi_rope, RET_DK ** -0.5), rope_args, BF16, "proj_rk")
    rv = proj(C_RV, 2048, _epi_plain, [], BF16, "proj_rv")
    rg = proj(C_RG, 2048, _epi_plain, [], F32, "proj_rg")
    mq = proj(C_MQ, 1024, functools.partial(_epi_headnorm, MOBA_DH), norm_arg(moba_q_norm), F32, "proj_mq")
    def proj_groups(col0, epi, args, name):
        return [_matmul(xn, w_in, col0, 1024, tm, TN, epi, args, [F32], name + sfx, row0=r0, n_rows=nr)
                for sfx, tm, r0, nr in (("_p", SEQ // 8, 0, SEQ), ("_s", ROWS_S, SEQ, ROWS_S))]

    mk_p, mk_s = proj_groups(C_MK, functools.partial(_epi_headnorm, MOBA_DH), norm_arg(moba_k_norm), "proj_mk")
    mv_p, mv_s = proj_groups(C_MV, _epi_plain, [], "proj_mv")
    cq = proj(C_CQ, 1024, functools.partial(_epi_headnorm, MEM_DH), norm_arg(mem_q_norm), F32, "proj_cq")
    gates = proj(C_GATE, 3 * D_MODEL, _epi_sigmoid, [], BF16, "proj_gates")

    mn = _rmsnorm_cast(mem_prompt[0], mem_norm_w, N_MEM)
    mem_k_p = _matmul(mn, w_mem_kv, 0, hd_mem, N_MEM, TN, functools.partial(_epi_headnorm, MEM_DH),
                      norm_arg(mem_k_norm), [F32], "mem_k")
    mem_v_p = _matmul(mn, w_mem_kv, hd_mem, hd_mem, N_MEM, TN, _epi_plain, [], [F32], "mem_v")

    gn_w = ret_gn_w.reshape(1, -1)
    ret_o, ret_state_p = _retention_prompt(log_g, rq, rk, rv, rg, gn_w)
    ret_o, ret_state_s = _retention_sample(log_g, rq, rk, rv, rg, gn_w, state_ret, ret_o)

    moba_o, kmean_s = _moba_prompt(mq, mk_p, mv_p, cache_k, page_table)
    ids = _sample_block_select(mq, kmean_s)
    ids = ids[:, :, :DEC_SEQ, :MOBA_TOPK].transpose(0, 2, 1, 3).reshape(-1)
    moba_o = _sample_attend(mq, mk_s, mv_s, cache_k, cache_v, page_table, ids, moba_o)

    mem_o = _mem_attend_prompt(cq, mem_k_p, mem_v_p)
    mem_o = _mem_attend_sample(cq, cache_mem_k, cache_mem_v, mem_o)

    tn = TN_EPI
    merged = _gated_branch_merge([ret_o, moba_o, mem_o],
                                 [w_ret_o.astype(BF16), w_moba_o.astype(BF16), w_mem_o.astype(BF16)], gates)
    hres = _matmul(merged, w_out, 0, D_MODEL, TM, tn, _epi_residual,
                   [(x, (TM, tn), lambda i, j: (i, j))], [F32], "out_proj")
    hn = _rmsnorm_cast(hres, mlp_norm_w, 384)
    u2 = _matmul(hn, w_up, 0, D_FF, TM, TN, _epi_relu_sq, [], [BF16], "mlp_up")
    w_down_b = w_down.astype(BF16)
    y_p, y_s = [_matmul(u2, w_down_b, 0, D_MODEL, tm, tn, _epi_residual,
                        [(hres, (tm, tn), lambda i, j: (i, j))], [F32], "mlp_down" + sfx, row0=r0, n_rows=nr)
                for sfx, tm, r0, nr in (("_p", SEQ // 16, 0, SEQ), ("_s", ROWS_S, SEQ, ROWS_S))]

    def sample_rows(a):
        return a.reshape(DEC_BATCH, DEC_PAD, *a.shape[1:])[:, :DEC_SEQ]

    shape4 = (MOBA_HEADS, MOBA_DH)
    return (
        y_p[None],
        sample_rows(y_s),
        mk_p.reshape(1, SEQ, *shape4),
        mv_p.reshape(1, SEQ, *shape4),
        ret_state_p[None],
        mem_k_p.reshape(1, N_MEM, MEM_HEADS, MEM_DH),
        mem_v_p.reshape(1, N_MEM, MEM_HEADS, MEM_DH),
        sample_rows(mk_s).reshape(DEC_BATCH, DEC_SEQ, *shape4),
        sample_rows(mv_s).reshape(DEC_BATCH, DEC_SEQ, *shape4),
        ret_state_s,
    )
```

```python
import functools

import jax
import jax.numpy as jnp
from jax import lax
from jax.experimental import pallas as pl
from jax.experimental.pallas import tpu as pltpu

F32 = jnp.float32
BF16 = jnp.bfloat16

D_MODEL = 2048
SEQ = 8192
DEC_BATCH = 32
DEC_SEQ = 4
DEC_PAD = 8
PAST_LEN = 16384
PAGE_SIZE = 128
N_MEM = 256
RET_HEADS = 8
RET_DK = 128
RET_DV = 256
RET_CHUNK = 128
ROPE_BASE = 10000.0
MOBA_HEADS = 8
MOBA_DH = 128
MOBA_BLOCK = 256
MOBA_TOPK = 3
MEM_HEADS = 4
MEM_DH = 256
D_FF = 4 * D_MODEL
EPS = 1e-6
GN_EPS = 1e-5
NEG = -1e30
LOG2_E = 1.4426950408889634

ROWS_S = DEC_BATCH * DEC_PAD
ROWS = SEQ + ROWS_S
TM = 1056
TN = 1024
TN_EPI = 512
LANES = 128
SEQS_PER_STEP = 2
VMEM_LIMIT = 56 * 1024 * 1024

C_RQ, C_RK, C_RV, C_RG = 0, 1024, 2048, 4096
C_MQ, C_MK, C_MV, C_CQ, C_GATE = 6144, 7168, 8192, 9216, 10240

_NT = (((1,), (1,)), ((), ()))


def _iota(shape, dim):
    return lax.broadcasted_iota(jnp.int32, shape, dim)


def _params(*sem):
    return pltpu.CompilerParams(dimension_semantics=sem, vmem_limit_bytes=VMEM_LIMIT)


def _zero_slab(width):
    return jnp.zeros((ROWS, width), BF16)


def _rmsnorm_cast(x, w, tm):
    rows, d = x.shape

    def body(x_ref, w_ref, o_ref):
        xf = x_ref[...]
        y = xf * lax.rsqrt(jnp.mean(xf * xf, axis=-1, keepdims=True) + EPS)
        o_ref[...] = (y * w_ref[...]).astype(o_ref.dtype)

    return pl.pallas_call(
        body,
        grid=(rows // tm,),
        in_specs=[pl.BlockSpec((tm, d), lambda i: (i, 0)), pl.BlockSpec((1, d), lambda i: (0, 0))],
        out_specs=pl.BlockSpec((tm, d), lambda i: (i, 0)),
        out_shape=jax.ShapeDtypeStruct((rows, d), BF16),
        compiler_params=_params("arbitrary"),
        name="rmsnorm_cast",
    )(x, w.reshape(1, d))


def _stack_rmsnorm(xp, xs, w):
    tm = ROWS_S
    n_p = SEQ // tm
    d = xp.shape[1]

    def body(xp_ref, xs_ref, w_ref, x_ref, o_ref):
        xf = jnp.where(pl.program_id(0) < n_p, xp_ref[...], xs_ref[...])
        x_ref[...] = xf
        y = xf * lax.rsqrt(jnp.mean(xf * xf, axis=-1, keepdims=True) + EPS)
        o_ref[...] = (y * w_ref[...]).astype(o_ref.dtype)

    return pl.pallas_call(
        body,
        grid=(n_p + 1,),
        in_specs=[pl.BlockSpec((tm, d), lambda i: (jnp.minimum(i, n_p - 1), 0)),
                  pl.BlockSpec((tm, d), lambda i: (0, 0)),
                  pl.BlockSpec((1, d), lambda i: (0, 0))],
        out_specs=[pl.BlockSpec((tm, d), lambda i: (i, 0))] * 2,
        out_shape=[jax.ShapeDtypeStruct((ROWS, d), F32), jax.ShapeDtypeStruct((ROWS, d), BF16)],
        compiler_params=_params("arbitrary"),
        name="stack_rmsnorm",
    )(xp, xs, w.reshape(1, d))


def _rope_tables(inv):
    tm = TM

    def body(inv_ref, c_ref, s_ref):
        row = pl.program_id(0) * tm + _iota((tm, LANES), 0)
        pos = jnp.where(row < SEQ, row, PAST_LEN + ((row - SEQ) & (DEC_PAD - 1)))
        ang = pos.astype(F32) * inv_ref[...]
        sn = jnp.sin(ang)
        c_ref[...] = jnp.cos(ang)
        s_ref[...] = jnp.where(_iota((tm, LANES), 1) < RET_DK // 2, -sn, sn)

    return pl.pallas_call(
        body,
        grid=(ROWS // tm,),
        in_specs=[pl.BlockSpec((1, LANES), lambda i: (0, 0))],
        out_specs=[pl.BlockSpec((tm, LANES), lambda i: (i, 0))] * 2,
        out_shape=[jax.ShapeDtypeStruct((ROWS, LANES), F32)] * 2,
        compiler_params=_params("arbitrary"),
        name="rope_tables",
    )(inv)


def _matmul(a, w, col0, n_cols, tm, tn, epi_fn, epi_args, out_dtypes, name, row0=0, n_rows=None):
    k = a.shape[1]
    m = a.shape[0] if n_rows is None else n_rows
    assert m % tm == 0 and row0 % tm == 0 and n_cols % tn == 0 and col0 % tn == 0
    jb0, ib0 = col0 // tn, row0 // tm
    n_epi, n_out = len(epi_args), len(out_dtypes)
    cast_once = w.dtype != BF16

    def body(a_ref, w_ref, *rest):
        epi_refs, out_refs = rest[:n_epi], rest[n_epi:n_epi + n_out]
        if cast_once:
            wb_ref = rest[-1]

            @pl.when(pl.program_id(1) == 0)
            def _():
                wb_ref[...] = w_ref[...].astype(BF16)
        else:
            wb_ref = w_ref
        acc = jnp.dot(a_ref[...], wb_ref[...], preferred_element_type=F32)
        for o_ref, o in zip(out_refs, epi_fn(acc, *epi_refs)):
            o_ref[...] = o.astype(o_ref.dtype)

    def spec(block, index_map):
        if cast_once:
            return pl.BlockSpec(block, lambda j, i: index_map(i, j))
        return pl.BlockSpec(block, index_map)

    in_specs = [spec((tm, k), lambda i, j: (ib0 + i, 0)), spec((k, tn), lambda i, j: (0, jb0 + j))]
    in_specs += [spec(bs, lambda i, j, im=im: im(ib0 + i, j)) for _, bs, im in epi_args]
    n_i, n_j = m // tm, n_cols // tn
    outs = pl.pallas_call(
        body,
        grid=(n_j, n_i) if cast_once else (n_i, n_j),
        in_specs=in_specs,
        out_specs=[spec((tm, tn), lambda i, j: (i, j)) for _ in out_dtypes],
        out_shape=[jax.ShapeDtypeStruct((m, n_cols), dt) for dt in out_dtypes],
        scratch_shapes=[pltpu.VMEM((k, tn), BF16)] if cast_once else [],
        compiler_params=_params("arbitrary", "arbitrary"),
        name=name,
    )(a, w, *[x for x, _, _ in epi_args])
    return outs[0] if n_out == 1 else outs


def _epi_plain(acc):
    return (acc,)


def _epi_sigmoid(acc):
    return (1.0 / (1.0 + jnp.exp(-acc)),)


def _epi_relu_sq(acc):
    u = jnp.maximum(acc, 0.0)
    return (u * u,)


def _epi_rope(scale, acc, c_ref, s_ref):
    c, s = c_ref[...], s_ref[...]
    pieces = []
    for hh in range(acc.shape[1] // RET_DK):
        x = acc[:, hh * RET_DK:(hh + 1) * RET_DK]
        y = x * c + pltpu.roll(x, RET_DK // 2, 1) * s
        pieces.append(y * scale if scale != 1.0 else y)
    return (jnp.concatenate(pieces, axis=1),)


def _epi_headnorm(hw, acc, w_ref):
    w = w_ref[...]
    pieces = []
    for hh in range(acc.shape[1] // hw):
        x = acc[:, hh * hw:(hh + 1) * hw]
        pieces.append(x * lax.rsqrt(jnp.mean(x * x, axis=-1, keepdims=True) + EPS) * w)
    return (jnp.concatenate(pieces, axis=1),)


def _epi_residual(acc, x_ref):
    return (x_ref[...] + acc,)


def _gated_branch_merge(branches, weights, gates):
    n_b = len(branches)
    tm, tn = TM, TN_EPI

    def body(*refs):
        a_refs, w_refs, g_refs, o_ref = refs[:n_b], refs[n_b:2 * n_b], refs[2 * n_b:3 * n_b], refs[3 * n_b]
        acc = None
        for a_ref, w_ref, g_ref in zip(a_refs, w_refs, g_refs):
            term = g_ref[...].astype(F32) * jnp.dot(a_ref[...], w_ref[...], preferred_element_type=F32)
            acc = term if acc is None else acc + term
        o_ref[...] = acc.astype(o_ref.dtype)

    in_specs = [pl.BlockSpec((tm, a.shape[1]), lambda i, j: (i, 0)) for a in branches]
    in_specs += [pl.BlockSpec((w.shape[0], tn), lambda i, j: (0, j)) for w in weights]
    in_specs += [pl.BlockSpec((tm, tn), lambda i, j, b=b: (i, b * (D_MODEL // tn) + j)) for b in range(n_b)]
    return pl.pallas_call(
        body,
        grid=(ROWS // tm, D_MODEL // tn),
        in_specs=in_specs,
        out_specs=pl.BlockSpec((tm, tn), lambda i, j: (i, j)),
        out_shape=jax.ShapeDtypeStruct((ROWS, D_MODEL), BF16),
        compiler_params=_params("arbitrary", "arbitrary"),
        name="branch_merge",
    )(*branches, *weights, *([gates] * n_b))


def _gated_groupnorm(o, rg, gw):
    mu = jnp.mean(o, axis=-1, keepdims=True)
    d = o - mu
    var = jnp.mean(d * d, axis=-1, keepdims=True)
    rn = d * lax.rsqrt(var + GN_EPS) * gw
    return rg / (1.0 + jnp.exp(-rg)) * rn


def _retention_prompt(log_g, rq, rk, rv, rg, gn_w):
    C = RET_CHUNK
    nc = SEQ // C
    HP = RET_HEADS

    def body(lg_ref, q_ref, k_ref, v_ref, rg_ref, gw_ref, _, o_ref, st_ref, state, dmask_t, qdec_t, kdec_t):
        hp, c = pl.program_id(0), pl.program_id(1)

        @pl.when(c == 0)
        def _():
            state[...] = jnp.zeros_like(state)
            diff = (_iota((C, C), 0) - _iota((C, C), 1)).astype(F32)
            ii = _iota((C, 1), 0).astype(F32)
            for hh in range(HP):
                lg = lg_ref[hp * HP + hh]
                dmask_t[hh] = jnp.where(diff >= 0, jnp.exp(jnp.maximum(diff, 0.0) * lg), 0.0)
                qdec_t[hh] = jnp.broadcast_to(jnp.exp((ii + 1.0) * lg), (C, RET_DV))
                kdec_t[hh] = jnp.broadcast_to(jnp.exp((C - 1.0 - ii) * lg), (C, RET_DK))

        for hh in range(HP):
            lg = lg_ref[hp * HP + hh]
            q = q_ref[:, hh * RET_DK:(hh + 1) * RET_DK]
            k = k_ref[:, hh * RET_DK:(hh + 1) * RET_DK]
            v = v_ref[:, hh * RET_DV:(hh + 1) * RET_DV]
            s = lax.dot_general(q, k, _NT, preferred_element_type=F32) * dmask_t[hh]
            c_decay = jnp.exp(jnp.zeros((1, RET_DV), F32) + C * lg)
            s_old = state[hh]
            o = (jnp.dot(s.astype(BF16), v, preferred_element_type=F32)
                 + jnp.dot(q, s_old.astype(BF16), preferred_element_type=F32) * qdec_t[hh])
            kd_t = (k.astype(F32) * kdec_t[hh]).T.astype(BF16)
            s_new = s_old * c_decay + jnp.dot(kd_t, v, preferred_element_type=F32)
            state[hh] = s_new
            cols = slice(hh * RET_DV, (hh + 1) * RET_DV)
            o_ref[:, cols] = _gated_groupnorm(o, rg_ref[:, cols], gw_ref[:, cols]).astype(o_ref.dtype)

        @pl.when(c == nc - 1)
        def _():
            st_ref[...] = state[...]

    return pl.pallas_call(
        body,
        grid=(RET_HEADS // HP, nc),
        in_specs=[
            pl.BlockSpec(memory_space=pltpu.SMEM),
            pl.BlockSpec((C, HP * RET_DK), lambda h, c: (c, h)),
            pl.BlockSpec((C, HP * RET_DK), lambda h, c: (c, h)),
            pl.BlockSpec((C, HP * RET_DV), lambda h, c: (c, h)),
            pl.BlockSpec((C, HP * RET_DV), lambda h, c: (c, h)),
            pl.BlockSpec((1, HP * RET_DV), lambda h, c: (0, h)),
            pl.BlockSpec(memory_space=pl.ANY),
        ],
        out_specs=[
            pl.BlockSpec((C, HP * RET_DV), lambda h, c: (c, h)),
            pl.BlockSpec((HP, RET_DK, RET_DV), lambda h, c: (h, 0, 0)),
        ],
        out_shape=[
            jax.ShapeDtypeStruct((ROWS, RET_HEADS * RET_DV), BF16),
            jax.ShapeDtypeStruct((RET_HEADS, RET_DK, RET_DV), F32),
        ],
        input_output_aliases={6: 0},
        scratch_shapes=[
            pltpu.VMEM((HP, RET_DK, RET_DV), F32),
            pltpu.VMEM((HP, C, C), F32),
            pltpu.VMEM((HP, C, RET_DV), F32),
            pltpu.VMEM((HP, C, RET_DK), F32),
        ],
        compiler_params=_params("arbitrary", "arbitrary"),
        name="retention_prompt",
    )(log_g, rq, rk, rv, rg, gn_w, _zero_slab(RET_HEADS * RET_DV))


def _retention_sample(log_g, rq, rk, rv, rg, gn_w, state_ret, ret_o):
    T = DEC_SEQ
    P = DEC_PAD
    nb = SEQS_PER_STEP
    row_blk0 = SEQ // (nb * P)

    def body(lg_ref, q_ref, k_ref, v_ref, rg_ref, gw_ref, st_in_ref, _, o_ref, st_ref):
        qa, ka, va = q_ref[...].astype(F32), k_ref[...].astype(F32), v_ref[...].astype(F32)
        rga, gw = rg_ref[...], gw_ref[...]
        ri, ci = _iota((P, P), 0), _iota((P, P), 1)
        diff = (ri - ci).astype(F32)
        ii = _iota((P, 1), 0)
        iif = ii.astype(F32)
        live = ii < T
        out_rows = []
        for bb in range(nb):
            rows = slice(bb * P, (bb + 1) * P)
            out_heads = []
            for h in range(RET_HEADS):
                lg = lg_ref[h]
                q = qa[rows, h * RET_DK:(h + 1) * RET_DK]
                k = jnp.where(live, ka[rows, h * RET_DK:(h + 1) * RET_DK], 0.0)
                v = jnp.where(live, va[rows, h * RET_DV:(h + 1) * RET_DV], 0.0)
                dmask = jnp.where(diff >= 0, jnp.exp(jnp.maximum(diff, 0.0) * lg), 0.0)
                s = lax.dot_general(q, k, _NT, preferred_element_type=F32) * dmask
                q_decay = jnp.exp((iif + 1.0) * lg)
                k_decay = jnp.exp((T - 1.0 - iif) * lg)
                c_decay = jnp.exp(jnp.zeros((1, RET_DV), F32) + T * lg)
                s_old = st_in_ref[bb, h]
                o = (jnp.dot(s, v, preferred_element_type=F32)
                     + jnp.dot(q, s_old, preferred_element_type=F32) * q_decay)
                upd = lax.dot_general(k * k_decay, v, (((0,), (0,)), ((), ())),
                                      preferred_element_type=F32)
                st_ref[bb, h] = s_old * c_decay + upd
                out_heads.append(_gated_groupnorm(
                    o, rga[rows, h * RET_DV:(h + 1) * RET_DV], gw[:, h * RET_DV:(h + 1) * RET_DV]))
            out_rows.append(jnp.concatenate(out_heads, axis=1))
        o_ref[...] = jnp.concatenate(out_rows, axis=0).astype(o_ref.dtype)

    wq, wv = RET_HEADS * RET_DK, RET_HEADS * RET_DV
    return pl.pallas_call(
        body,
        grid=(DEC_BATCH // nb,),
        in_specs=[
            pl.BlockSpec(memory_space=pltpu.SMEM),
            pl.BlockSpec((nb * P, wq), lambda g: (row_blk0 + g, 0)),
            pl.BlockSpec((nb * P, wq), lambda g: (row_blk0 + g, 0)),
            pl.BlockSpec((nb * P, wv), lambda g: (row_blk0 + g, 0)),
            pl.BlockSpec((nb * P, wv), lambda g: (row_blk0 + g, 0)),
            pl.BlockSpec((1, wv), lambda g: (0, 0)),
            pl.BlockSpec((nb, RET_HEADS, RET_DK, RET_DV), lambda g: (g, 0, 0, 0)),
            pl.BlockSpec(memory_space=pl.ANY),
        ],
        out_specs=[
            pl.BlockSpec((nb * P, wv), lambda g: (row_blk0 + g, 0)),
            pl.BlockSpec((nb, RET_HEADS, RET_DK, RET_DV), lambda g: (g, 0, 0, 0)),
        ],
        out_shape=[
            jax.ShapeDtypeStruct((ROWS, wv), BF16),
            jax.ShapeDtypeStruct((DEC_BATCH, RET_HEADS, RET_DK, RET_DV), F32),
        ],
        input_output_aliases={7: 0},
        compiler_params=_params("arbitrary"),
        name="retention_sample",
    )(log_g, rq, rk, rv, rg, gn_w, state_ret, ret_o)


def _moba_prompt(mq, mk, mv, cache_k, page_table):
    BL = MOBA_BLOCK
    nblk = SEQ // BL
    TQ = 2 * BL
    scale = MOBA_DH ** -0.5

    n_sblk = PAST_LEN // MOBA_BLOCK
    ppb = MOBA_BLOCK // PAGE_SIZE
    UNITS = DEC_BATCH * n_sblk
    n_tiles = SEQ // TQ
    trips_per_head = (n_tiles // 2) ** 2
    U_TILE, U_TRIP = 4, 3
    assert MOBA_HEADS * (n_tiles * U_TILE + trips_per_head * U_TRIP) == UNITS
    AHEAD = 12
    RING = AHEAD + max(U_TILE, U_TRIP)
    assert RING & (RING - 1) == 0 and UNITS & (UNITS - 1) == 0 and n_sblk & (n_sblk - 1) == 0
    sblk_shift = n_sblk.bit_length() - 1

    def body(pt_ref, q_ref, k_ref, v_ref, _, ck_ref, o_ref, km_ref, kaug, vt, kmean, s_a, s_b, p_a, p_b,
             ring, sem):
        h, ti = pl.program_id(0), pl.program_id(1)

        def unit_copies(u):
            uu = u & (UNITS - 1)
            b, j = uu >> sblk_shift, uu & (n_sblk - 1)
            slot = u & (RING - 1)
            return [pltpu.make_async_copy(ck_ref.at[pt_ref[b, j * ppb + pg]], ring.at[slot, pg],
                                          sem.at[slot, pg]) for pg in range(ppb)]

        def fetch(u0, n):
            for i in range(n):
                for cp in unit_copies(u0 + AHEAD + i):
                    cp.start()

        def reduce(u0, n):
            for i in range(n):
                u = u0 + i
                for cp in unit_copies(u):
                    cp.wait()
                slot = u & (RING - 1)
                lanes8 = (8, PAGE_SIZE // 8, MOBA_HEADS, MOBA_DH)
                parts = [jnp.sum(ring[slot, pg].reshape(lanes8), axis=1) for pg in range(ppb)]
                tot = jnp.sum(functools.reduce(lambda a, b: a + b, parts), axis=0)
                km_ref[u >> sblk_shift, u & (n_sblk - 1)] = tot * (1.0 / MOBA_BLOCK)

        @pl.when(jnp.logical_and(h == 0, ti == 0))
        def _():
            fetch(-AHEAD, AHEAD)

        u_tile = (h * n_tiles + ti) * U_TILE + (h * trips_per_head + ((ti * ti) >> 2)) * U_TRIP
        fetch(u_tile, U_TILE)
        reduce(u_tile, U_TILE)

        @pl.when(ti == 0)
        def _():
            kmean[...] = jnp.zeros_like(kmean)
            lane = _iota((BL, LANES), 1)

            def prep(jb, carry):
                r0 = pl.multiple_of(jb * BL, BL)
                kb = k_ref[pl.ds(r0, BL), :]
                kmean[pl.ds(jb, 1), :] = jnp.mean(kb, axis=0, keepdims=True)
                kaug[pl.ds(r0, BL), 0:MOBA_DH] = kb.astype(BF16)
                kaug[pl.ds(r0, BL), MOBA_DH:2 * MOBA_DH] = (lane == jb).astype(BF16)
                vt[:, pl.ds(r0, BL)] = v_ref[pl.ds(r0, BL), :].T.astype(BF16)
                return carry

            lax.fori_loop(0, nblk, prep, 0)

        q = q_ref[...]
        gt = lax.dot_general(kmean[0:nblk, :], q, _NT, precision=lax.Precision.HIGHEST,
                             preferred_element_type=F32)
        n_sg = nblk // 8
        row = _iota((nblk, TQ), 0)
        row8 = _iota((8, TQ), 0)
        second = jnp.where(_iota((8, TQ), 1) >= BL, 1, 0)
        own = 2 * ti + second
        gtv = jnp.where(row < 2 * ti + jnp.where(_iota((nblk, TQ), 1) >= BL, 1, 0), gt, -jnp.inf)
        cnts = [jnp.zeros((8, TQ), jnp.int32) for _ in range(n_sg)]
        for blk in range(nblk):
            g_blk = gtv[blk:blk + 1, :]
            for sg in range(n_sg):
                g_sg = gt[8 * sg:8 * sg + 8, :]
                if 8 * sg > blk:
                    cnts[sg] = cnts[sg] + jnp.where(g_blk >= g_sg, 1, 0)
                elif 8 * sg + 7 < blk:
                    cnts[sg] = cnts[sg] + jnp.where(g_blk > g_sg, 1, 0)
                else:
                    cnts[sg] = cnts[sg] + jnp.where(row8 + 8 * sg > blk, jnp.where(g_blk >= g_sg, 1, 0),
                                                    jnp.where(g_blk > g_sg, 1, 0))
        picked = [(cnts[sg] < MOBA_TOPK) & (row8 + 8 * sg < own) for sg in range(n_sg)]
        qs = (q * (scale * LOG2_E)).astype(BF16)

        parts = [jnp.where(kp, 0.0, NEG) for kp in picked] + [jnp.zeros((LANES - nblk, TQ), F32)]
        bias = jnp.concatenate(parts, axis=0).T
        blk_lane = _iota((TQ, LANES), 1)
        own_row = 2 * ti + jnp.where(_iota((TQ, LANES), 0) >= BL, 1, 0)

        def with_bias(b):
            return jnp.concatenate([qs, b.astype(BF16)], axis=1)

        qa_past = with_bias(jnp.where(blk_lane < 2 * ti, bias, NEG))
        qa_own = with_bias(jnp.where(blk_lane == own_row, 0.0, bias))

        def scores(qx, g):
            c0 = pl.multiple_of(g * TQ, TQ)
            return lax.dot_general(kaug[pl.ds(c0, TQ), :], qx, _NT, preferred_element_type=F32)

        def values(g):
            c0 = pl.multiple_of(g * TQ, TQ)
            return vt[:, pl.ds(c0, TQ)]

        CH = 64

        def fold8(x, fn):
            return fn(x.reshape(CH // 8, 8, TQ), axis=0)

        def softmax_stage(s_ref, p_ref, m, l):
            mx8 = jnp.broadcast_to(m, (8, TQ))
            for c in range(TQ // CH):
                mx8 = jnp.maximum(mx8, fold8(s_ref[c * CH:(c + 1) * CH, :], jnp.max))
            m_new = jnp.max(mx8, axis=0, keepdims=True)
            alpha = jnp.exp2(m - m_new)
            sum8 = jnp.zeros((8, TQ), F32)
            for c in range(TQ // CH):
                pc = jnp.exp2(s_ref[c * CH:(c + 1) * CH, :] - m_new)
                sum8 = sum8 + fold8(pc, jnp.sum)
                p_ref[c * CH:(c + 1) * CH, :] = pc.astype(BF16)
            return m_new, l * alpha + jnp.sum(sum8, axis=0, keepdims=True), alpha

        last = SEQ // TQ - 1

        s_a[...] = jnp.where(_iota((TQ, TQ), 0) <= _iota((TQ, TQ), 1), scores(qa_own, ti), NEG)
        m, l, _ = softmax_stage(s_a, p_a, jnp.full((1, TQ), NEG, F32), jnp.zeros((1, TQ), F32))
        s_b[...] = scores(qa_past, 0)
        acc = jnp.zeros((MOBA_DH, TQ), F32)

        def trip(k, carry):
            m, l, acc = carry
            g0 = 2 * k
            u_trip = u_tile + U_TILE + k * U_TRIP
            fetch(u_trip, U_TRIP)
            s_a[...] = scores(qa_past, jnp.minimum(g0 + 1, last))
            pv = jnp.dot(values(jnp.where(k == 0, ti, g0 - 1)), p_a[...], preferred_element_type=F32)
            m, l, alpha = softmax_stage(s_b, p_b, m, l)
            acc = (acc + pv) * alpha
            s_b[...] = scores(qa_past, jnp.minimum(g0 + 2, last))
            pv = jnp.dot(values(g0), p_b[...], preferred_element_type=F32)
            m, l, alpha = softmax_stage(s_a, p_a, m, l)
            reduce(u_trip, U_TRIP)
            return m, l, (acc + pv) * alpha

        trips = (ti + 1) >> 1
        m, l, acc = lax.fori_loop(0, trips, trip, (m, l, acc))
        acc = acc + jnp.dot(values(jnp.where(trips == 0, ti, 2 * trips - 1)), p_a[...],
                            preferred_element_type=F32)
        o_ref[...] = (acc / l).T.astype(o_ref.dtype)

        @pl.when(jnp.logical_and(h == MOBA_HEADS - 1, ti == n_tiles - 1))
        def _():
            for u in range(UNITS, UNITS + AHEAD):
                for cp in unit_copies(u):
                    cp.wait()

    grid_spec = pltpu.PrefetchScalarGridSpec(
        num_scalar_prefetch=1,
        grid=(MOBA_HEADS, n_tiles),
        in_specs=[
            pl.BlockSpec((TQ, MOBA_DH), lambda h, i, pt: (i, h)),
            pl.BlockSpec((SEQ, MOBA_DH), lambda h, i, pt: (0, h)),
            pl.BlockSpec((SEQ, MOBA_DH), lambda h, i, pt: (0, h)),
            pl.BlockSpec(memory_space=pl.ANY),
            pl.BlockSpec(memory_space=pl.ANY),
        ],
        out_specs=[
            pl.BlockSpec((TQ, MOBA_DH), lambda h, i, pt: (i, h)),
            pl.BlockSpec((DEC_BATCH, n_sblk, MOBA_HEADS, MOBA_DH), lambda h, i, pt: (0, 0, 0, 0)),
        ],
        scratch_shapes=[
            pltpu.VMEM((SEQ, 2 * MOBA_DH), BF16),
            pltpu.VMEM((MOBA_DH, SEQ), BF16),
            pltpu.VMEM((LANES, MOBA_DH), F32),
            pltpu.VMEM((TQ, TQ), F32),
            pltpu.VMEM((TQ, TQ), F32),
            pltpu.VMEM((TQ, TQ), BF16),
            pltpu.VMEM((TQ, TQ), BF16),
            pltpu.VMEM((RING, ppb, PAGE_SIZE, MOBA_HEADS, MOBA_DH), F32),
            pltpu.SemaphoreType.DMA((RING, ppb)),
        ],
    )
    return pl.pallas_call(
        body,
        grid_spec=grid_spec,
        out_shape=[
            jax.ShapeDtypeStruct((ROWS, MOBA_HEADS * MOBA_DH), BF16),
            jax.ShapeDtypeStruct((DEC_BATCH, n_sblk, MOBA_HEADS, MOBA_DH), F32),
        ],
        input_output_aliases={4: 0},
        compiler_params=_params("arbitrary", "arbitrary"),
        name="moba_prompt",
    )(page_table, mq, mk, mv, _zero_slab(MOBA_HEADS * MOBA_DH), cache_k)


def _sample_block_select(mq, kmean):
    nblk = PAST_LEN // MOBA_BLOCK
    P = DEC_PAD
    nb = 4
    row_blk0 = SEQ // (nb * P)

    def body(q_ref, km_ref, o_ref):
        q = q_ref[...]
        lane = _iota((P, LANES), 1)
        for bb in range(nb):
            for h in range(MOBA_HEADS):
                cols = slice(h * MOBA_DH, (h + 1) * MOBA_DH)
                kmh = jnp.concatenate([km_ref[bb, :, h, :], jnp.zeros((LANES - nblk, MOBA_DH), F32)], axis=0)
                gate = lax.dot_general(q[bb * P:(bb + 1) * P, cols], kmh, _NT, precision=lax.Precision.HIGHEST,
                                       preferred_element_type=F32)
                g = jnp.where(lane < nblk, gate, -jnp.inf)
                ids = []
                for _ in range(MOBA_TOPK):
                    mx = jnp.max(g, axis=-1, keepdims=True)
                    first = jnp.min(jnp.where(g == mx, lane, 1 << 20), axis=-1, keepdims=True)
                    ids.append(first)
                    g = jnp.where(lane == first, -jnp.inf, g)
                o_ref[bb, h] = jnp.where(lane == 0, ids[0], jnp.where(lane == 1, ids[1], ids[2]))

    return pl.pallas_call(
        body,
        grid=(DEC_BATCH // nb,),
        in_specs=[
            pl.BlockSpec((nb * P, MOBA_HEADS * MOBA_DH), lambda b: (row_blk0 + b, 0)),
            pl.BlockSpec((nb, nblk, MOBA_HEADS, MOBA_DH), lambda b: (b, 0, 0, 0)),
        ],
        out_specs=pl.BlockSpec((nb, MOBA_HEADS, P, LANES), lambda b: (b, 0, 0, 0)),
        out_shape=jax.ShapeDtypeStruct((DEC_BATCH, MOBA_HEADS, P, LANES), jnp.int32),
        compiler_params=_params("arbitrary"),
        name="sample_block_select",
    )(mq, kmean)


def _sample_attend(mq, mk, mv, cache_k, cache_v, page_table, blk_ids, moba_o):
    ppb = MOBA_BLOCK // PAGE_SIZE
    n_sel = MOBA_TOPK * MOBA_BLOCK
    nb = SEQS_PER_STEP
    n_q = nb * DEC_SEQ
    n_copies = n_q * MOBA_TOPK * ppb
    n_steps = (DEC_BATCH // nb) * MOBA_HEADS
    scale = MOBA_DH ** -0.5
    P = DEC_PAD
    row_blk0 = SEQ // (nb * P)

    def body(pt_ref, ids_ref, q_ref, kn_ref, vn_ref, ck_ref, cv_ref, _, o_ref, kbuf, vbuf, sem):
        step = pl.program_id(0) * MOBA_HEADS + pl.program_id(1)

        def copies(st, slot):
            g, h = st // MOBA_HEADS, st % MOBA_HEADS
            out = []
            for qi in range(n_q):
                b, t = g * nb + qi // DEC_SEQ, qi % DEC_SEQ
                for sl in range(MOBA_TOPK):
                    blk = ids_ref[((b * DEC_SEQ + t) * MOBA_HEADS + h) * MOBA_TOPK + sl]
                    for pg in range(ppb):
                        phys = pt_ref[b, blk * ppb + pg]
                        n = (qi * MOBA_TOPK + sl) * ppb + pg
                        rows = pl.ds((sl * ppb + pg) * PAGE_SIZE, PAGE_SIZE)
                        for src, dst, which in ((ck_ref, kbuf, 0), (cv_ref, vbuf, 1)):
                            out.append(pltpu.make_async_copy(src.at[phys, :, h, :], dst.at[slot, qi, rows, :],
                                                             sem.at[slot, which, n]))
            return out

        slot = step % 2

        @pl.when(step == 0)
        def _():
            for n, cp in enumerate(copies(step, 0)):
                cp.start(priority=n % 2)

        @pl.when(step + 1 < n_steps)
        def _():
            for n, cp in enumerate(copies(step + 1, 1 - slot)):
                cp.start(priority=n % 2)

        for cp in copies(step, slot):
            cp.wait()

        q, kn, vn = q_ref[...], kn_ref[...], vn_ref[...]
        rown = _iota((P, 1), 0)
        out_rows = []
        for bb in range(nb):
            knb, vnb = kn[bb * P:(bb + 1) * P], vn[bb * P:(bb + 1) * P]
            res = jnp.zeros((P, MOBA_DH), F32)
            for t in range(DEC_SEQ):
                qi = bb * DEC_SEQ + t
                qt = q[bb * P + t:bb * P + t + 1, :] * scale
                s = jnp.sum(kbuf[slot, qi] * qt, axis=-1, keepdims=True)
                sn = jnp.where(rown <= t, jnp.sum(knb * qt, axis=-1, keepdims=True), NEG)
                m = jnp.maximum(jnp.max(s, axis=0, keepdims=True), jnp.max(sn, axis=0, keepdims=True))
                p, pn = jnp.exp(s - m), jnp.exp(sn - m)
                l = jnp.sum(p, axis=0, keepdims=True) + jnp.sum(pn, axis=0, keepdims=True)
                o = (jnp.sum(p * vbuf[slot, qi], axis=0, keepdims=True)
                     + jnp.sum(pn * vnb, axis=0, keepdims=True))
                res = jnp.where(rown == t, o / l, res)
            out_rows.append(res)
        o_ref[...] = jnp.concatenate(out_rows, axis=0).astype(o_ref.dtype)

    head_blk = pl.BlockSpec((nb * P, MOBA_DH), lambda g, h, pt, ids: (row_blk0 + g, h))
    new_blk = pl.BlockSpec((nb * P, MOBA_DH), lambda g, h, pt, ids: (g, h))
    any_spec = pl.BlockSpec(memory_space=pl.ANY)
    grid_spec = pltpu.PrefetchScalarGridSpec(
        num_scalar_prefetch=2,
        grid=(DEC_BATCH // nb, MOBA_HEADS),
        in_specs=[head_blk, new_blk, new_blk, any_spec, any_spec, any_spec],
        out_specs=head_blk,
        scratch_shapes=[
            pltpu.VMEM((2, n_q, n_sel, MOBA_DH), F32),
            pltpu.VMEM((2, n_q, n_sel, MOBA_DH), F32),
            pltpu.SemaphoreType.DMA((2, 2, n_copies)),
        ],
    )
    return pl.pallas_call(
        body,
        grid_spec=grid_spec,
        out_shape=jax.ShapeDtypeStruct((ROWS, MOBA_HEADS * MOBA_DH), BF16),
        input_output_aliases={7: 0},
        compiler_params=_params("arbitrary", "arbitrary"),
        name="sample_attend",
    )(page_table, blk_ids, mq, mk, mv, cache_k, cache_v, moba_o)


def _softmax_attend(q, k, v):
    s = lax.dot_general(q, k, _NT, preferred_element_type=F32)
    m = jnp.max(s, axis=-1, keepdims=True)
    p = jnp.exp(s - m)
    l = jnp.sum(p, axis=-1, keepdims=True)
    return jnp.dot(p.astype(v.dtype), v, preferred_element_type=F32) / l


def _mem_attend_prompt(cq, mem_k, mem_v):
    tq = 1024
    w = MEM_HEADS * MEM_DH
    scale = MEM_DH ** -0.5

    def body(q_ref, k_ref, v_ref, _, o_ref):
        for h in range(MEM_HEADS):
            cols = slice(h * MEM_DH, (h + 1) * MEM_DH)
            q = (q_ref[:, cols] * scale).astype(BF16)
            o_ref[:, cols] = _softmax_attend(q, k_ref[:, cols].astype(BF16),
                                             v_ref[:, cols].astype(BF16)).astype(o_ref.dtype)

    kv_spec = pl.BlockSpec((N_MEM, w), lambda r: (0, 0))
    return pl.pallas_call(
        body,
        grid=(SEQ // tq,),
        in_specs=[pl.BlockSpec((tq, w), lambda r: (r, 0)), kv_spec, kv_spec, pl.BlockSpec(memory_space=pl.ANY)],
        out_specs=pl.BlockSpec((tq, w), lambda r: (r, 0)),
        out_shape=jax.ShapeDtypeStruct((ROWS, w), BF16),
        input_output_aliases={3: 0},
        compiler_params=_params("arbitrary"),
        name="mem_attend_prompt",
    )(cq, mem_k, mem_v, _zero_slab(w))


def _mem_attend_sample(cq, mem_k, mem_v, mem_o):
    nb = SEQS_PER_STEP
    P = DEC_PAD
    w = MEM_HEADS * MEM_DH
    scale = MEM_DH ** -0.5
    row_blk0 = SEQ // (nb * P)

    def body(q_ref, k_ref, v_ref, _, o_ref):
        qa = q_ref[...] * scale
        out_rows = []
        for bb in range(nb):
            heads = []
            for h in range(MEM_HEADS):
                cols = slice(h * MEM_DH, (h + 1) * MEM_DH)
                heads.append(_softmax_attend(qa[bb * P:(bb + 1) * P, cols], k_ref[bb, :, h, :],
                                             v_ref[bb, :, h, :]))
            out_rows.append(jnp.concatenate(heads, axis=1))
        o_ref[...] = jnp.concatenate(out_rows, axis=0).astype(o_ref.dtype)

    kv_spec = pl.BlockSpec((nb, N_MEM, MEM_HEADS, MEM_DH), lambda g: (g, 0, 0, 0))
    row_spec = pl.BlockSpec((nb * P, w), lambda g: (row_blk0 + g, 0))
    return pl.pallas_call(
        body,
        grid=(DEC_BATCH // nb,),
        in_specs=[row_spec, kv_spec, kv_spec, pl.BlockSpec(memory_space=pl.ANY)],
        out_specs=row_spec,
        out_shape=jax.ShapeDtypeStruct((ROWS, w), BF16),
        input_output_aliases={3: 0},
        compiler_params=_params("arbitrary"),
        name="mem_attend_sample",
    )(cq, mem_k, mem_v, mem_o)


def kernel(x_prompt, x_sample, cache_k, cache_v, cache_mem_k, cache_mem_v, state_ret, page_table, mem_prompt, attn_norm_w, w_in, moba_q_norm, moba_k_norm, mem_q_norm, mem_k_norm, mem_norm_w, w_mem_kv, ret_gn_w, w_ret_o, w_moba_o, w_mem_o, w_out, mlp_norm_w, w_up, w_down):
    hd_mem = MEM_HEADS * MEM_DH

    xs = jnp.pad(x_sample, ((0, 0), (0, DEC_PAD - DEC_SEQ), (0, 0))).reshape(ROWS_S, D_MODEL)
    x, xn = _stack_rmsnorm(x_prompt[0], xs, attn_norm_w)

    half = RET_DK // 2
    inv = ROPE_BASE ** (-jnp.arange(half, dtype=F32) * (2.0 / RET_DK))
    inv = jnp.concatenate([inv, inv]).reshape(1, LANES)
    log_g = jnp.log1p(-jnp.exp2(-5.0 - jnp.arange(RET_HEADS, dtype=F32)))
    cos_t, sin_t = _rope_tables(inv)

    rope_args = [(cos_t, (TM, LANES), lambda i, j: (i, 0)), (sin_t, (TM, LANES), lambda i, j: (i, 0))]

    def norm_arg(wv):
        return [(wv.reshape(1, -1), (1, wv.shape[0]), lambda i, j: (0, 0))]

    def proj(col0, n_cols, epi, args, dt, name):
        return _matmul(xn, w_in, col0, n_cols, TM, TN, epi, args, [dt], name)

    rq = proj(C_RQ, 1024, functools.partial(_epi_rope, 1.0), rope_args, BF16, "proj_rq")
    rk = proj(C_RK, 1024, functools.partial(_epi_rope, RET_DK ** -0.5), rope_args, BF16, "proj_rk")
    rv = proj(C_RV, 2048, _epi_plain, [], BF16, "proj_rv")
    rg = proj(C_RG, 2048, _epi_plain, [], F32, "proj_rg")
    mq = proj(C_MQ, 1024, functools.partial(_epi_headnorm, MOBA_DH), norm_arg(moba_q_norm), F32, "proj_mq")
    def proj_groups(col0, epi, args, name):
        return [_matmul(xn, w_in, col0, 1024, tm, TN, epi, args, [F32], name + sfx, row0=r0, n_rows=nr)
                for sfx, tm, r0, nr in (("_p", SEQ // 8, 0, SEQ), ("_s", ROWS_S, SEQ, ROWS_S))]

    mk_p, mk_s = proj_groups(C_MK, functools.partial(_epi_headnorm, MOBA_DH), norm_arg(moba_k_norm), "proj_mk")
    mv_p, mv_s = proj_groups(C_MV, _epi_plain, [], "proj_mv")
    cq = proj(C_CQ, 1024, functools.partial(_epi_headnorm, MEM_DH), norm_arg(mem_q_norm), F32, "proj_cq")
    gates = proj(C_GATE, 3 * D_MODEL, _epi_sigmoid, [], BF16, "proj_gates")

    mn = _rmsnorm_cast(mem_prompt[0], mem_norm_w, N_MEM)
    mem_k_p = _matmul(mn, w_mem_kv, 0, hd_mem, N_MEM, TN, functools.partial(_epi_headnorm, MEM_DH),
                      norm_arg(mem_k_norm), [F32], "mem_k")
    mem_v_p = _matmul(mn, w_mem_kv, hd_mem, hd_mem, N_MEM, TN, _epi_plain, [], [F32], "mem_v")

    gn_w = ret_gn_w.reshape(1, -1)
    ret_o, ret_state_p = _retention_prompt(log_g, rq, rk, rv, rg, gn_w)
    ret_o, ret_state_s = _retention_sample(log_g, rq, rk, rv, rg, gn_w, state_ret, ret_o)

    moba_o, kmean_s = _moba_prompt(mq, mk_p, mv_p, cache_k, page_table)
    ids = _sample_block_select(mq, kmean_s)
    ids = ids[:, :, :DEC_SEQ, :MOBA_TOPK].transpose(0, 2, 1, 3).reshape(-1)
    moba_o = _sample_attend(mq, mk_s, mv_s, cache_k, cache_v, page_table, ids, moba_o)

    mem_o = _mem_attend_prompt(cq, mem_k_p, mem_v_p)
    mem_o = _mem_attend_sample(cq, cache_mem_k, cache_mem_v, mem_o)

    tn = TN_EPI
    merged = _gated_branch_merge([ret_o, moba_o, mem_o],
                                 [w_ret_o.astype(BF16), w_moba_o.astype(BF16), w_mem_o.astype(BF16)], gates)
    hres = _matmul(merged, w_out, 0, D_MODEL, TM, tn, _epi_residual,
                   [(x, (TM, tn), lambda i, j: (i, j))], [F32], "out_proj")
    hn = _rmsnorm_cast(hres, mlp_norm_w, 384)
    u2 = _matmul(hn, w_up, 0, D_FF, TM, TN, _epi_relu_sq, [], [BF16], "mlp_up")
    w_down_b = w_down.astype(BF16)
    y_p, y_s = [_matmul(u2, w_down_b, 0, D_MODEL, tm, tn, _epi_residual,
                        [(hres, (tm, tn), lambda i, j: (i, j))], [F32], "mlp_down" + sfx, row0=r0, n_rows=nr)
                for sfx, tm, r0, nr in (("_p", SEQ // 16, 0, SEQ), ("_s", ROWS_S, SEQ, ROWS_S))]

    def sample_rows(a):
        return a.reshape(DEC_BATCH, DEC_PAD, *a.shape[1:])[:, :DEC_SEQ]

    shape4 = (MOBA_HEADS, MOBA_DH)
    return (
        y_p[None],
        sample_rows(y_s),
        mk_p.reshape(1, SEQ, *shape4),
        mv_p.reshape(1, SEQ, *shape4),
        ret_state_p[None],
        mem_k_p.reshape(1, N_MEM, MEM_HEADS, MEM_DH),
        mem_v_p.reshape(1, N_MEM, MEM_HEADS, MEM_DH),
        sample_rows(mk_s).reshape(DEC_BATCH, DEC_SEQ, *shape4),
        sample_rows(mv_s).reshape(DEC_BATCH, DEC_SEQ, *shape4),
        ret_state_s,
    )
```
